```python
import jax, jax.numpy as jnp
from jax import lax
import numpy as np

D_MODEL = 1024
BATCH = 4
SEQ = 8192
DEPTH = 2
DEC_BATCH = 1
DEC_SEQ = 16384
PAST_LEN = 128

HEAD_DIM = 64
D_MIX = D_MODEL
A_HEADS = 6
A_KV_HEADS = 2
A_GROUP = A_HEADS // A_KV_HEADS
B_GROUPS = 4
C_HEADS = 6
A_Q = A_HEADS * HEAD_DIM
A_KV = A_KV_HEADS * HEAD_DIM
B_W = B_GROUPS * HEAD_DIM
C_W = C_HEADS * HEAD_DIM
IN_SPLIT = (A_Q, A_KV, A_KV, B_W, B_W, C_W, C_W, C_W)
D_IN = sum(IN_SPLIT)
BLOCK = 128
WINDOW = 128
CHUNK = 128
GRID_W = 64
NA_KH = 8
NA_KW = 16
ROPE_THETA = 500000.0
ROPE_DIMS = HEAD_DIM // 4
D_FF = 4 * D_MODEL
EPS = 1e-6
ATTN_SCALE = HEAD_DIM ** -0.5

kernel_name = 'hybrid_parallel_heads_encoder'


def _rmsnorm(x, g):
    xf = x.astype(jnp.float32)
    y = xf * lax.rsqrt(jnp.mean(xf * xf, axis=-1, keepdims=True) + EPS)
    return (y * g.astype(jnp.float32)).astype(x.dtype)


def _rope(x):
    S = x.shape[1]
    half = ROPE_DIMS // 2
    inv = ROPE_THETA ** (-jnp.arange(0, ROPE_DIMS, 2, dtype=jnp.float32) / ROPE_DIMS)
    ang = jnp.arange(S, dtype=jnp.float32)[:, None] * inv[None, :]
    cos = jnp.cos(ang)[None, :, None, :]
    sin = jnp.sin(ang)[None, :, None, :]
    xr = x[..., :ROPE_DIMS].astype(jnp.float32)
    x1, x2 = xr[..., :half], xr[..., half:]
    rot = jnp.concatenate([x1 * cos - x2 * sin, x2 * cos + x1 * sin], axis=-1)
    return jnp.concatenate([rot.astype(x.dtype), x[..., ROPE_DIMS:]], axis=-1)


def _window_gqa(q, k, v, sink):
    Bn, S = q.shape[0], q.shape[1]
    nb = S // BLOCK
    pad = ((0, 0), (BLOCK, BLOCK), (0, 0), (0, 0))
    kp = jnp.pad(k, pad).reshape(Bn, nb + 2, BLOCK, A_KV_HEADS, HEAD_DIM)
    vp = jnp.pad(v, pad).reshape(Bn, nb + 2, BLOCK, A_KV_HEADS, HEAD_DIM)
    kb = jnp.concatenate([kp[:, :-2], kp[:, 1:-1], kp[:, 2:]], axis=2)
    vb = jnp.concatenate([vp[:, :-2], vp[:, 1:-1], vp[:, 2:]], axis=2)
    qb = q.reshape(Bn, nb, BLOCK, A_KV_HEADS, A_GROUP, HEAD_DIM)
    s = jnp.einsum('bnqhgd,bnkhd->bnhgqk', qb, kb,
                   preferred_element_type=jnp.float32) * ATTN_SCALE
    qi = jnp.arange(BLOCK)[:, None]
    kj = jnp.arange(3 * BLOCK)[None, :]
    in_win = jnp.abs(kj - BLOCK - qi) <= WINDOW
    kabs = jnp.arange(nb)[:, None, None] * BLOCK + kj[None] - BLOCK
    mask = in_win[None] & (kabs >= 0) & (kabs < S)
    s = jnp.where(mask[None, :, None, None], s, -jnp.inf)
    sink_col = jnp.broadcast_to(
        sink.astype(jnp.float32).reshape(1, 1, A_KV_HEADS, A_GROUP, 1, 1), s.shape[:-1] + (1,))
    p = jax.nn.softmax(jnp.concatenate([s, sink_col], axis=-1), axis=-1)[..., :-1]
    o = jnp.einsum('bnhgqk,bnkhd->bnqhgd', p.astype(v.dtype), vb)
    return o.reshape(Bn, S, A_Q)


def _spatial_gate(u, v, g_v, w_s, b_s):
    Bn, S = u.shape[0], u.shape[1]
    nc = S // CHUNK
    vr = _rmsnorm(v, g_v).reshape(Bn, nc, CHUNK, B_GROUPS, HEAD_DIM)
    sv = jnp.einsum('gpq,bcqge->bcpge', w_s.astype(vr.dtype), vr) + b_s.T[None, None, :, :, None].astype(vr.dtype)
    return u * sv.reshape(Bn, S, B_W)


def _neighbourhood_attn(q, k, v, rel_bias):
    Bn, S = q.shape[0], q.shape[1]
    rows = S // GRID_W
    kh = min(NA_KH, rows)
    r = jnp.arange(rows)
    rs = jnp.clip(r - kh // 2, 0, rows - kh)
    row_idx = rs[:, None] + jnp.arange(kh)[None, :]
    col = jnp.arange(GRID_W)
    cs = jnp.clip(col - NA_KW // 2, 0, GRID_W - NA_KW)
    qg = q.reshape(Bn, rows, GRID_W, C_HEADS, HEAD_DIM)
    kg = k.reshape(Bn, rows, GRID_W, C_HEADS, HEAD_DIM)[:, row_idx]
    vg = v.reshape(Bn, rows, GRID_W, C_HEADS, HEAD_DIM)[:, row_idx]
    s = jnp.einsum('brqhd,brkwhd->brhqkw', qg, kg,
                   preferred_element_type=jnp.float32) * ATTN_SCALE
    dr = row_idx - r[:, None] + NA_KH - 1
    dc = jnp.clip(col[None, :] - col[:, None] + NA_KW - 1, 0, 2 * NA_KW - 2)
    bias = rel_bias[:, dr[:, :, None, None], dc[None, None]]
    bias = bias.transpose(1, 0, 3, 2, 4).astype(jnp.float32)
    col_in = (col[None, :] >= cs[:, None]) & (col[None, :] < cs[:, None] + NA_KW)
    s = jnp.where(col_in[:, None, :], s + bias[None], -jnp.inf)
    sh = s.shape
    p = jax.nn.softmax(s.reshape(sh[:4] + (kh * GRID_W,)), axis=-1).reshape(sh)
    o = jnp.einsum('brhqkw,brkwhd->brqhd', p.astype(v.dtype), vg)
    return o.reshape(Bn, S, C_W)


def _layer(x, c, norm1_g, norm2_g, w_ada, b_ada, w_in, qa_norm_g, ka_norm_g, a_sink,
           bv_norm_g, w_spatial, b_spatial, qc_norm_g, kc_norm_g, c_rel_bias,
           w_out, w_ff1, w_ff2):
    Bn, S = x.shape[0], x.shape[1]
    mod = (jax.nn.silu(c) @ w_ada + b_ada)[:, None, :]
    shift1, scale1, gate1, shift2, scale2, gate2 = jnp.split(mod, 6, axis=-1)
    h = _rmsnorm(x, norm1_g) * (1 + scale1) + shift1
    proj = h @ w_in
    aq, ak, av, bu, bv, cq, ck, cv = jnp.split(proj, list(np.cumsum(IN_SPLIT)[:-1]), axis=-1)
    aq = _rope(_rmsnorm(aq.reshape(Bn, S, A_HEADS, HEAD_DIM), qa_norm_g))
    ak = _rope(_rmsnorm(ak.reshape(Bn, S, A_KV_HEADS, HEAD_DIM), ka_norm_g))
    av = av.reshape(Bn, S, A_KV_HEADS, HEAD_DIM)
    out_a = _window_gqa(aq, ak, av, a_sink)
    out_b = _spatial_gate(jax.nn.gelu(bu), jax.nn.gelu(bv), bv_norm_g, w_spatial, b_spatial)
    cq = _rmsnorm(cq.reshape(Bn, S, C_HEADS, HEAD_DIM), qc_norm_g)
    ck = _rmsnorm(ck.reshape(Bn, S, C_HEADS, HEAD_DIM), kc_norm_g)
    cv = cv.reshape(Bn, S, C_HEADS, HEAD_DIM)
    out_c = _neighbourhood_attn(cq, ck, cv, c_rel_bias)
    mix = jnp.concatenate([out_a, out_b, out_c], axis=-1) @ w_out
    x = x + gate1 * mix
    h2 = _rmsnorm(x, norm2_g) * (1 + scale2) + shift2
    ff = jnp.square(jax.nn.relu(h2 @ w_ff1)) @ w_ff2
    return x + gate2 * ff


def _trunk(x, c, norm1_g, norm2_g, w_ada, b_ada, w_in, qa_norm_g, ka_norm_g, a_sink,
           bv_norm_g, w_spatial, b_spatial, qc_norm_g, kc_norm_g, c_rel_bias,
           w_out, w_ff1, w_ff2):
    for l in range(DEPTH):
        x = _layer(x, c, norm1_g[l], norm2_g[l], w_ada[l], b_ada[l], w_in[l],
                   qa_norm_g[l], ka_norm_g[l], a_sink[l], bv_norm_g[l], w_spatial[l],
                   b_spatial[l], qc_norm_g[l], kc_norm_g[l], c_rel_bias[l],
                   w_out[l], w_ff1[l], w_ff2[l])
    return x


def setup_inputs(seed: int = 0) -> dict:
    key = jax.random.key(seed)
    ks = jax.random.split(key, 24)
    f32 = jnp.float32

    def nrm(k, shape, s):
        return s * jax.random.normal(k, shape, f32)

    return {
        'x_prompt': nrm(ks[0], (BATCH, SEQ, D_MODEL), 1.0),
        'x_sample': nrm(ks[1], (DEC_BATCH, DEC_SEQ, D_MODEL), 1.0),
        'c_prompt': nrm(ks[2], (BATCH, D_MODEL), 1.0),
        'c_sample': nrm(ks[3], (DEC_BATCH, D_MODEL), 1.0),
        'norm1_g': 1.0 + nrm(ks[4], (DEPTH, D_MODEL), 0.02),
        'norm2_g': 1.0 + nrm(ks[5], (DEPTH, D_MODEL), 0.02),
        'w_ada': nrm(ks[6], (DEPTH, D_MODEL, 6 * D_MODEL), 0.5 * D_MODEL ** -0.5),
        'b_ada': nrm(ks[7], (DEPTH, 6 * D_MODEL), 0.02),
        'w_in': nrm(ks[8], (DEPTH, D_MODEL, D_IN), D_MODEL ** -0.5),
        'qa_norm_g': 1.0 + nrm(ks[9], (DEPTH, HEAD_DIM), 0.02),
        'ka_norm_g': 1.0 + nrm(ks[10], (DEPTH, HEAD_DIM), 0.02),
        'a_sink': nrm(ks[11], (DEPTH, A_HEADS), 1.0),
        'bv_norm_g': 1.0 + nrm(ks[12], (DEPTH, B_W), 0.02),
        'w_spatial': nrm(ks[13], (DEPTH, B_GROUPS, CHUNK, CHUNK), 0.5 * CHUNK ** -0.5),
        'b_spatial': 1.0 + nrm(ks[14], (DEPTH, B_GROUPS, CHUNK), 0.02),
        'qc_norm_g': 1.0 + nrm(ks[15], (DEPTH, HEAD_DIM), 0.02),
        'kc_norm_g': 1.0 + nrm(ks[16], (DEPTH, HEAD_DIM), 0.02),
        'c_rel_bias': nrm(ks[17], (DEPTH, C_HEADS, 2 * NA_KH - 1, 2 * NA_KW - 1), 0.5),
        'w_out': nrm(ks[18], (DEPTH, D_MIX, D_MODEL), D_MIX ** -0.5),
        'w_ff1': nrm(ks[19], (DEPTH, D_MODEL, D_FF), D_MODEL ** -0.5),
        'w_ff2': nrm(ks[20], (DEPTH, D_FF, D_MODEL), D_FF ** -0.5),
    }


def reference(x_prompt, x_sample, c_prompt, c_sample, norm1_g, norm2_g, w_ada, b_ada,
              w_in, qa_norm_g, ka_norm_g, a_sink, bv_norm_g, w_spatial, b_spatial,
              qc_norm_g, kc_norm_g, c_rel_bias, w_out, w_ff1, w_ff2):
    y_prompt = _trunk(x_prompt, c_prompt, norm1_g, norm2_g, w_ada, b_ada, w_in,
                      qa_norm_g, ka_norm_g, a_sink, bv_norm_g, w_spatial, b_spatial,
                      qc_norm_g, kc_norm_g, c_rel_bias, w_out, w_ff1, w_ff2)
    y_sample = _trunk(x_sample, c_sample, norm1_g, norm2_g, w_ada, b_ada, w_in,
                      qa_norm_g, ka_norm_g, a_sink, bv_norm_g, w_spatial, b_spatial,
                      qc_norm_g, kc_norm_g, c_rel_bias, w_out, w_ff1, w_ff2)
    return (y_prompt, y_sample)
```

```python
from functools import partial

import numpy as np
import jax
import jax.numpy as jnp
from jax import lax
from jax.experimental import pallas as pl
from jax.experimental.pallas import tpu as pltpu

D_MODEL = 1024
HEAD_DIM = 64
A_HEADS = 6
A_KV_HEADS = 2
A_GROUP = A_HEADS // A_KV_HEADS
B_GROUPS = 4
C_HEADS = 6
A_Q = A_HEADS * HEAD_DIM
A_KV = A_KV_HEADS * HEAD_DIM
B_W = B_GROUPS * HEAD_DIM
C_W = C_HEADS * HEAD_DIM
D_IN = A_Q + 2 * A_KV + 2 * B_W + 3 * C_W
D_MIX = A_Q + B_W + C_W
BLOCK = 128
CHUNK = 128
GRID_W = 64
NA_KH = 8
NA_KW = 16
ROPE_THETA = 500000.0
ROPE_DIMS = HEAD_DIM // 4
D_FF = 4 * D_MODEL
EPS = 1e-6
ATTN_SCALE = HEAD_DIM ** -0.5

LANES_V7X = 128
VMEM_LIMIT_BYTES_V7X = 56 * 1024 * 1024

TM_IN = 512
TM_FFN = 512
FF_CHUNK = 1024
TQ_WIN = 512
TR_NBR = 8
KV_ROWS_BLK = 4
ADA_TN = 1536

NEG_INF = float("-inf")

A_HEAD_ORDER = (0, 3, 1, 4, 2, 5)


def _cparams(n_axes):
    return pltpu.CompilerParams(
        dimension_semantics=("arbitrary",) * n_axes,
        vmem_limit_bytes=VMEM_LIMIT_BYTES_V7X,
    )


def _lane_is_low(shape):
    lane = lax.broadcasted_iota(jnp.int32, shape, len(shape) - 1)
    return (lane % LANES_V7X) < HEAD_DIM


def _ada_kernel(c_ref, w_ref, b_ref, o_ref):
    c = c_ref[...]
    s = (c * jax.nn.sigmoid(c)).astype(jnp.bfloat16)
    w = w_ref[...].astype(jnp.bfloat16)
    o_ref[...] = jnp.dot(s, w, preferred_element_type=jnp.float32) + b_ref[...]


def _ada_call(c8, w_ada, b_ada):
    depth = w_ada.shape[0]
    n = w_ada.shape[2]
    return pl.pallas_call(
        _ada_kernel,
        grid=(depth, n // ADA_TN),
        in_specs=[
            pl.BlockSpec((8, D_MODEL), lambda l, j: (0, 0)),
            pl.BlockSpec((None, D_MODEL, ADA_TN), lambda l, j: (l, 0, j)),
            pl.BlockSpec((None, 1, ADA_TN), lambda l, j: (l, 0, j)),
        ],
        out_specs=pl.BlockSpec((None, 8, ADA_TN), lambda l, j: (l, 0, j)),
        out_shape=jax.ShapeDtypeStruct((depth, 8, n), jnp.float32),
        compiler_params=_cparams(2),
        name="ada_mod",
    )(c8, w_ada, b_ada.reshape(depth, 1, n))


def _head_rms(x, seg_ones, gain):
    sq = (x * x).astype(jnp.bfloat16)
    ssum = jnp.dot(sq, seg_ones, preferred_element_type=jnp.float32)
    return (x * lax.rsqrt(ssum * (1.0 / HEAD_DIM) + EPS)) * gain


def _rope(x, cos_t, sin_up, sin_dn):
    n = x.shape[1] // LANES_V7X
    outs = []
    for j in range(n):
        xj = x[:, j * LANES_V7X:(j + 1) * LANES_V7X]
        up = pltpu.roll(xj, LANES_V7X - ROPE_DIMS // 2, 1)
        dn = pltpu.roll(xj, ROPE_DIMS // 2, 1)
        outs.append(xj * cos_t + up * sin_up + dn * sin_dn)
    return outs[0] if n == 1 else jnp.concatenate(outs, axis=1)


def _inproj_kernel(x_ref, mod_ref, g1_ref, w_ref, seg_ref, gqa_ref, gka_ref, gqc_ref, gkc_ref,
                   gbv_ref, cos_ref, sup_ref, sdn_ref, ws_ref, bs_ref,
                   aq_ref, ak_ref, av_ref, ob_ref, cq_ref, ck_ref, cv_ref):
    x = x_ref[...]
    shift1 = mod_ref[0:1, :]
    scale1 = mod_ref[1:2, :]
    ms = jnp.mean(x * x, axis=-1, keepdims=True)
    h = ((x * lax.rsqrt(ms + EPS)) * g1_ref[...]) * (1.0 + scale1) + shift1
    proj = jnp.dot(h.astype(jnp.bfloat16), w_ref[...], preferred_element_type=jnp.float32)

    seg = seg_ref[...]
    cos_t, sin_up, sin_dn = cos_ref[...], sup_ref[...], sdn_ref[...]
    o = 0
    aq = proj[:, o:o + A_Q]; o += A_Q
    ak = proj[:, o:o + A_KV]; o += A_KV
    av = proj[:, o:o + A_KV]; o += A_KV
    bu = proj[:, o:o + B_W]; o += B_W
    bv = proj[:, o:o + B_W]; o += B_W
    cq = proj[:, o:o + C_W]; o += C_W
    ck = proj[:, o:o + C_W]; o += C_W
    cv = proj[:, o:o + C_W]

    aq_n = jnp.concatenate(
        [_head_rms(aq[:, j * 128:(j + 1) * 128], seg, gqa_ref[...]) for j in range(A_Q // 128)], axis=1)
    aq_ref[...] = _rope(aq_n, cos_t, sin_up, sin_dn).astype(jnp.bfloat16)
    ak_ref[...] = _rope(_head_rms(ak, seg, gka_ref[...]), cos_t, sin_up, sin_dn).astype(jnp.bfloat16)
    av_ref[...] = av.astype(jnp.bfloat16)

    cq_ref[...] = jnp.concatenate(
        [_head_rms(cq[:, j * 128:(j + 1) * 128], seg, gqc_ref[...]) for j in range(C_W // 128)],
        axis=1).astype(jnp.bfloat16)
    ck_ref[...] = jnp.concatenate(
        [_head_rms(ck[:, j * 128:(j + 1) * 128], seg, gkc_ref[...]) for j in range(C_W // 128)],
        axis=1).astype(jnp.bfloat16)
    cv_ref[...] = cv.astype(jnp.bfloat16)

    u = jax.nn.gelu(bu)
    v = jax.nn.gelu(bv)
    vms = jnp.mean(v * v, axis=-1, keepdims=True)
    vr = ((v * lax.rsqrt(vms + EPS)) * gbv_ref[...]).astype(jnp.bfloat16)
    low = _lane_is_low((CHUNK, LANES_V7X))
    zero = jnp.zeros((), jnp.bfloat16)
    tm = x.shape[0]
    for c in range(tm // CHUNK):
        rows = slice(c * CHUNK, (c + 1) * CHUNK)
        sv_parts = []
        for j in range(B_W // LANES_V7X):
            vj = vr[rows, j * LANES_V7X:(j + 1) * LANES_V7X]
            sv = jnp.dot(ws_ref[2 * j], jnp.where(low, vj, zero), preferred_element_type=jnp.float32)
            sv += jnp.dot(ws_ref[2 * j + 1], jnp.where(low, zero, vj), preferred_element_type=jnp.float32)
            sv_parts.append(sv)
        sv = jnp.concatenate(sv_parts, axis=1) + bs_ref[...]
        ob_ref[rows, :] = (u[rows, :] * sv).astype(jnp.bfloat16)


def _inproj_call(x, mod, layer, boff, p):
    bsz, seq, _ = x.shape
    tm = TM_IN
    tok = lambda w: pl.BlockSpec((None, tm, w), lambda b, i: (b, i, 0))
    const2 = lambda a: pl.BlockSpec(a.shape, lambda b, i: (0, 0))
    bf = lambda w: jax.ShapeDtypeStruct((bsz, seq, w), jnp.bfloat16)
    rope_spec = pl.BlockSpec((tm, LANES_V7X), lambda b, i: (i, 0))
    return pl.pallas_call(
        _inproj_kernel,
        grid=(bsz, seq // tm),
        in_specs=[
            tok(D_MODEL),
            pl.BlockSpec((None, None, 6, D_MODEL), lambda b, i: (layer, boff + b, 0, 0)),
            const2(p["g1"]), const2(p["w_in"]), const2(p["seg"]),
            const2(p["gqa"]), const2(p["gka"]), const2(p["gqc"]), const2(p["gkc"]), const2(p["gbv"]),
            rope_spec, rope_spec, rope_spec,
            pl.BlockSpec(p["ws"].shape, lambda b, i: (0, 0, 0)),
            const2(p["bs"]),
        ],
        out_specs=[tok(A_Q), tok(A_KV), tok(A_KV), tok(B_W), tok(C_W), tok(C_W), tok(C_W)],
        out_shape=[bf(A_Q), bf(A_KV), bf(A_KV), bf(B_W), bf(C_W), bf(C_W), bf(C_W)],
        compiler_params=_cparams(2),
        name="inproj",
    )(x, mod, p["g1"], p["w_in"], p["seg"], p["gqa"], p["gka"], p["gqc"], p["gkc"], p["gbv"],
      p["cos"], p["sin_up"], p["sin_dn"], p["ws"], p["bs"])


def _softmax_pv(s, v, extra_logit=None):
    m = jnp.max(s, axis=-1, keepdims=True)
    if extra_logit is not None:
        m = jnp.maximum(m, extra_logit)
    e = jnp.exp(s - m)
    l = jnp.sum(e, axis=-1, keepdims=True)
    if extra_logit is not None:
        l = l + jnp.exp(extra_logit - m)
    o = jnp.dot(e.astype(jnp.bfloat16), v, preferred_element_type=jnp.float32)
    return o / l


def _window_kernel(sink_ref, band_ref, q_ref, kp_ref, kc_ref, kn_ref, vp_ref, vc_ref, vn_ref, o_ref):
    i = pl.program_id(1)
    last = pl.num_programs(1) - 1
    kwin = jnp.concatenate([kp_ref[...], kc_ref[...], kn_ref[...]], axis=0)
    vwin = jnp.concatenate([vp_ref[...], vc_ref[...], vn_ref[...]], axis=0)
    band = band_ref[...]
    col = lax.broadcasted_iota(jnp.int32, band.shape, 1)
    low = _lane_is_low((BLOCK, LANES_V7X))
    zero = jnp.zeros((), jnp.bfloat16)
    nt = TQ_WIN // BLOCK
    dn = (((1,), (1,)), ((), ()))
    no_prev = jnp.where(i == 0, NEG_INF, 0.0)
    no_next = jnp.where(i == last, NEG_INF, 0.0)
    for t in range(nt):
        mask = band
        if t == 0:
            mask = mask + jnp.where(col < BLOCK, no_prev, 0.0)
        if t == nt - 1:
            mask = mask + jnp.where(col >= 2 * BLOCK, no_next, 0.0)
        k = kwin[t * BLOCK:(t + 3) * BLOCK, :]
        v = vwin[t * BLOCK:(t + 3) * BLOCK, :]
        outs = []
        for g in range(A_Q // LANES_V7X):
            q = q_ref[t * BLOCK:(t + 1) * BLOCK, g * LANES_V7X:(g + 1) * LANES_V7X]
            s_lo = lax.dot_general(jnp.where(low, q, zero), k, dn, preferred_element_type=jnp.float32) + mask
            s_hi = lax.dot_general(jnp.where(low, zero, q), k, dn, preferred_element_type=jnp.float32) + mask
            o_lo = _softmax_pv(s_lo, v, sink_ref[2 * g])
            o_hi = _softmax_pv(s_hi, v, sink_ref[2 * g + 1])
            outs.append(jnp.where(low, o_lo, o_hi))
        o_ref[t * BLOCK:(t + 1) * BLOCK, :] = jnp.concatenate(outs, axis=1).astype(jnp.bfloat16)


def _window_call(aq, ak, av, sink, band):
    bsz, seq, _ = aq.shape
    nt = TQ_WIN // BLOCK
    nb = seq // BLOCK
    cur = lambda w: pl.BlockSpec((None, TQ_WIN, w), lambda b, i: (b, i, 0))
    prev = pl.BlockSpec((None, BLOCK, A_KV), lambda b, i: (b, jnp.maximum(i * nt - 1, 0), 0))
    nxt = pl.BlockSpec((None, BLOCK, A_KV), lambda b, i: (b, jnp.minimum(i * nt + nt, nb - 1), 0))
    return pl.pallas_call(
        _window_kernel,
        grid=(bsz, seq // TQ_WIN),
        in_specs=[
            pl.BlockSpec(memory_space=pltpu.SMEM),
            pl.BlockSpec(band.shape, lambda b, i: (0, 0)),
            cur(A_Q), prev, cur(A_KV), nxt, prev, cur(A_KV), nxt,
        ],
        out_specs=cur(A_Q),
        out_shape=jax.ShapeDtypeStruct((bsz, seq, A_Q), jnp.bfloat16),
        compiler_params=_cparams(2),
        name="window_attn",
    )(sink, band, aq, ak, ak, ak, av, av, av)


def _nbr_kernel(rows, bias_ref, q_ref, k0, k1, k2, k3, v0, v1, v2, v3, o_ref, kwin, vwin):
    i = pl.program_id(1)
    blk = KV_ROWS_BLK * GRID_W
    for n, (kr, vr) in enumerate(((k0, v0), (k1, v1), (k2, v2), (k3, v3))):
        kwin[n * blk:(n + 1) * blk, :] = kr[...]
        vwin[n * blk:(n + 1) * blk, :] = vr[...]
    r0 = i * TR_NBR
    win_start = _nbr_window_block(i, rows) * KV_ROWS_BLK
    low = _lane_is_low((GRID_W, LANES_V7X))
    zero = jnp.zeros((), jnp.bfloat16)
    dn = (((1,), (1,)), ((), ()))
    nkeys = NA_KH * GRID_W

    def row_body(t, carry):
        r = r0 + t
        rs = jnp.clip(r - NA_KH // 2, 0, rows - NA_KH)
        delta = r - rs
        off = pl.multiple_of((rs - win_start) * GRID_W, GRID_W)
        qoff = pl.multiple_of(t * GRID_W, GRID_W)
        outs = []
        for g in range(C_W // LANES_V7X):
            cols = slice(g * LANES_V7X, (g + 1) * LANES_V7X)
            q = q_ref[pl.ds(qoff, GRID_W), cols]
            k = kwin[pl.ds(off, nkeys), cols]
            v = vwin[pl.ds(off, nkeys), cols]
            s_lo = lax.dot_general(jnp.where(low, q, zero), k, dn, preferred_element_type=jnp.float32)
            s_hi = lax.dot_general(jnp.where(low, zero, q), k, dn, preferred_element_type=jnp.float32)
            o_lo = _softmax_pv(s_lo + bias_ref[2 * g, delta], v)
            o_hi = _softmax_pv(s_hi + bias_ref[2 * g + 1, delta], v)
            outs.append(jnp.where(low, o_lo, o_hi))
        o_ref[pl.ds(qoff, GRID_W), :] = jnp.concatenate(outs, axis=1).astype(jnp.bfloat16)
        return carry

    lax.fori_loop(0, TR_NBR, row_body, 0)


def _nbr_window_block(i, rows):
    per_tile = TR_NBR // KV_ROWS_BLK
    nblk = rows // KV_ROWS_BLK
    return jnp.clip(i * per_tile - 1, 0, nblk - (per_tile + 2))


def _nbr_call(cq, ck, cv, bias_tab):
    bsz, seq, _ = cq.shape
    rows = seq // GRID_W
    tq = TR_NBR * GRID_W
    blk = KV_ROWS_BLK * GRID_W
    nwin = TR_NBR // KV_ROWS_BLK + 2
    assert nwin == 4 and rows % TR_NBR == 0 and rows >= nwin * KV_ROWS_BLK and NA_KH // 2 <= KV_ROWS_BLK
    kv_spec = lambda n: pl.BlockSpec((None, blk, C_W), lambda b, i: (b, _nbr_window_block(i, rows) + n, 0))
    q_spec = pl.BlockSpec((None, tq, C_W), lambda b, i: (b, i, 0))
    kv_specs = [kv_spec(n) for n in range(nwin)]
    return pl.pallas_call(
        partial(_nbr_kernel, rows),
        grid=(bsz, rows // TR_NBR),
        in_specs=[pl.BlockSpec(bias_tab.shape, lambda b, i: (0, 0, 0, 0)), q_spec] + kv_specs + kv_specs,
        out_specs=q_spec,
        out_shape=jax.ShapeDtypeStruct((bsz, seq, C_W), jnp.bfloat16),
        scratch_shapes=[pltpu.VMEM((nwin * blk, C_W), jnp.bfloat16),
                        pltpu.VMEM((nwin * blk, C_W), jnp.bfloat16)],
        compiler_params=_cparams(2),
        name="nbr_attn",
    )(bias_tab, cq, ck, ck, ck, ck, cv, cv, cv, cv)


def _ffn_kernel(x_ref, oa_ref, ob_ref, oc_ref, mod_ref, g2_ref, wo_ref, w1_ref, w2_ref, y_ref):
    gate1 = mod_ref[2:3, :]
    shift2 = mod_ref[3:4, :]
    scale2 = mod_ref[4:5, :]
    gate2 = mod_ref[5:6, :]
    mix_in = jnp.concatenate([oa_ref[...], ob_ref[...], oc_ref[...]], axis=1)
    mix = jnp.dot(mix_in, wo_ref[...], preferred_element_type=jnp.float32)
    x1 = x_ref[...] + gate1 * mix
    ms = jnp.mean(x1 * x1, axis=-1, keepdims=True)
    h2 = (((x1 * lax.rsqrt(ms + EPS)) * g2_ref[...]) * (1.0 + scale2) + shift2).astype(jnp.bfloat16)
    ff = None
    for j in range(D_FF // FF_CHUNK):
        cols = slice(j * FF_CHUNK, (j + 1) * FF_CHUNK)
        a = jnp.maximum(jnp.dot(h2, w1_ref[:, cols], preferred_element_type=jnp.float32), 0.0)
        part = jnp.dot((a * a).astype(jnp.bfloat16), w2_ref[cols, :], preferred_element_type=jnp.float32)
        ff = part if ff is None else ff + part
    y_ref[...] = x1 + gate2 * ff


def _ffn_call(x, oa, ob, oc, mod, layer, boff, p):
    bsz, seq, _ = x.shape
    tm = TM_FFN
    tok = lambda w: pl.BlockSpec((None, tm, w), lambda b, i: (b, i, 0))
    const2 = lambda a: pl.BlockSpec(a.shape, lambda b, i: (0, 0), pipeline_mode=pl.Buffered(1))
    return pl.pallas_call(
        _ffn_kernel,
        grid=(bsz, seq // tm),
        in_specs=[
            tok(D_MODEL), tok(A_Q), tok(B_W), tok(C_W),
            pl.BlockSpec((None, None, 6, D_MODEL), lambda b, i: (layer, boff + b, 0, 0)),
            const2(p["g2"]), const2(p["w_out"]), const2(p["w_ff1"]), const2(p["w_ff2"]),
        ],
        out_specs=tok(D_MODEL),
        out_shape=jax.ShapeDtypeStruct((bsz, seq, D_MODEL), jnp.float32),
        compiler_params=_cparams(2),
        name="outproj_ffn",
    )(x, oa, ob, oc, mod, p["g2"], p["w_out"], p["w_ff1"], p["w_ff2"])


def _tile_heads(g, scale=1.0):
    return (jnp.tile(g.astype(jnp.float32), LANES_V7X // HEAD_DIM) * scale).reshape(1, LANES_V7X)


def _rope_tables(seq):
    half = ROPE_DIMS // 2
    inv = ROPE_THETA ** (-jnp.arange(0, ROPE_DIMS, 2, dtype=jnp.float32) / ROPE_DIMS)
    ang = jnp.arange(seq, dtype=jnp.float32)[:, None] * inv[None, :]
    cos, sin = jnp.cos(ang), jnp.sin(ang)
    one = jnp.ones((seq, HEAD_DIM - ROPE_DIMS), jnp.float32)
    zero_t = jnp.zeros((seq, HEAD_DIM - ROPE_DIMS), jnp.float32)
    zero_h = jnp.zeros((seq, half), jnp.float32)
    rep = LANES_V7X // HEAD_DIM
    cos_t = jnp.tile(jnp.concatenate([cos, cos, one], axis=1), (1, rep))
    sin_up = jnp.tile(jnp.concatenate([-sin, zero_h, zero_t], axis=1), (1, rep))
    sin_dn = jnp.tile(jnp.concatenate([zero_h, sin, zero_t], axis=1), (1, rep))
    return cos_t, sin_up, sin_dn


def _band_mask():
    qi = np.arange(BLOCK)[:, None]
    kj = np.arange(3 * BLOCK)[None, :]
    ok = np.abs(kj - BLOCK - qi) <= BLOCK
    return jnp.asarray(np.where(ok, 0.0, NEG_INF).astype(np.float32))


def _nbr_bias_table(rel_bias):
    col = np.arange(GRID_W)
    cs = np.clip(col - NA_KW // 2, 0, GRID_W - NA_KW)
    col_in = (col[None, :] >= cs[:, None]) & (col[None, :] < cs[:, None] + NA_KW)
    dc = np.clip(col[None, :] - col[:, None] + NA_KW - 1, 0, 2 * NA_KW - 2)
    delta = np.arange(NA_KH)[:, None]
    j = np.arange(NA_KH)[None, :]
    dr = j - delta + NA_KH - 1
    b = rel_bias.astype(jnp.float32)[:, dr[:, :, None, None], dc[None, None]]
    b = jnp.where(jnp.asarray(col_in)[None, None, None], b, NEG_INF)
    b = b.transpose(0, 1, 3, 2, 4)
    return b.reshape(C_HEADS, NA_KH, GRID_W, NA_KH * GRID_W)


def _layer_params(l, seq_tables, norm1_g, norm2_g, w_in, qa_norm_g, ka_norm_g, a_sink, bv_norm_g,
                  w_spatial, b_spatial, qc_norm_g, kc_norm_g, c_rel_bias, w_out, w_ff1, w_ff2):
    order = np.asarray(A_HEAD_ORDER)
    q_cols = (order[:, None] * HEAD_DIM + np.arange(HEAD_DIM)[None, :]).reshape(-1)
    in_cols = np.concatenate([q_cols, np.arange(A_Q, D_IN)])
    out_rows = np.concatenate([q_cols, np.arange(A_Q, D_MIX)])
    seg = np.kron(np.eye(LANES_V7X // HEAD_DIM), np.ones((HEAD_DIM, HEAD_DIM))).astype(np.float32)
    cos_t, sin_up, sin_dn = seq_tables
    return {
        "g1": norm1_g[l].reshape(1, D_MODEL),
        "g2": norm2_g[l].reshape(1, D_MODEL),
        "w_in": w_in[l][:, in_cols].astype(jnp.bfloat16),
        "w_out": w_out[l][out_rows, :].astype(jnp.bfloat16),
        "w_ff1": w_ff1[l].astype(jnp.bfloat16),
        "w_ff2": w_ff2[l].astype(jnp.bfloat16),
        "seg": jnp.asarray(seg, jnp.bfloat16),
        "gqa": _tile_heads(qa_norm_g[l], ATTN_SCALE),
        "gka": _tile_heads(ka_norm_g[l]),
        "gqc": _tile_heads(qc_norm_g[l], ATTN_SCALE),
        "gkc": _tile_heads(kc_norm_g[l]),
        "gbv": bv_norm_g[l].reshape(1, B_W),
        "cos": cos_t, "sin_up": sin_up, "sin_dn": sin_dn,
        "ws": w_spatial[l].astype(jnp.bfloat16),
        "bs": jnp.repeat(b_spatial[l].T, HEAD_DIM, axis=1),
        "sink": a_sink[l][order].astype(jnp.float32),
        "nbr_bias": _nbr_bias_table(c_rel_bias[l]),
    }


def _trunk(x, mod, boff, layer_params):
    for l, p in enumerate(layer_params):
        aq, ak, av, ob, cq, ck, cv = _inproj_call(x, mod, l, boff, p)
        oa = _window_call(aq, ak, av, p["sink"], p["band"])
        oc = _nbr_call(cq, ck, cv, p["nbr_bias"])
        x = _ffn_call(x, oa, ob, oc, mod, l, boff, p)
    return x


def kernel(x_prompt, x_sample, c_prompt, c_sample, norm1_g, norm2_g, w_ada, b_ada, w_in, qa_norm_g, ka_norm_g, a_sink, bv_norm_g, w_spatial, b_spatial, qc_norm_g, kc_norm_g, c_rel_bias, w_out, w_ff1, w_ff2):
    depth = w_in.shape[0]
    nb_prompt, seq_prompt, _ = x_prompt.shape
    nb_sample, seq_sample, _ = x_sample.shape
    assert nb_prompt + nb_sample <= 8
    for seq in (seq_prompt, seq_sample):
        assert seq % TM_IN == 0 and seq % TM_FFN == 0 and seq % TQ_WIN == 0 and seq // BLOCK >= 3

    c8 = jnp.zeros((8, D_MODEL), jnp.float32)
    c8 = c8.at[:nb_prompt].set(c_prompt).at[nb_prompt:nb_prompt + nb_sample].set(c_sample)
    mod = _ada_call(c8, w_ada, b_ada).reshape(depth, 8, 6, D_MODEL)

    seq_tables = _rope_tables(max(seq_prompt, seq_sample))
    band = _band_mask()
    layer_params = []
    for l in range(depth):
        p = _layer_params(l, seq_tables, norm1_g, norm2_g, w_in, qa_norm_g, ka_norm_g, a_sink,
                          bv_norm_g, w_spatial, b_spatial, qc_norm_g, kc_norm_g, c_rel_bias,
                          w_out, w_ff1, w_ff2)
        p["band"] = band
        layer_params.append(p)

    y_prompt = _trunk(x_prompt, mod, 0, layer_params)
    y_sample = _trunk(x_sample, mod, nb_prompt, layer_params)
    return (y_prompt, y_sample)
```

```python
from functools import partial

import numpy as np
import jax
import jax.numpy as jnp
from jax import lax
from jax.experimental import pallas as pl
from jax.experimental.pallas import tpu as pltpu

D_MODEL = 1024
HEAD_DIM = 64
A_HEADS = 6
A_KV_HEADS = 2
A_GROUP = A_HEADS // A_KV_HEADS
B_GROUPS = 4
C_HEADS = 6
A_Q = A_HEADS * HEAD_DIM
A_KV = A_KV_HEADS * HEAD_DIM
B_W = B_GROUPS * HEAD_DIM
C_W = C_HEADS * HEAD_DIM
D_IN = A_Q + 2 * A_KV + 2 * B_W + 3 * C_W
D_MIX = A_Q + B_W + C_W
BLOCK = 128
CHUNK = 128
GRID_W = 64
NA_KH = 8
NA_KW = 16
ROPE_THETA = 500000.0
ROPE_DIMS = HEAD_DIM // 4
D_FF = 4 * D_MODEL
EPS = 1e-6
ATTN_SCALE = HEAD_DIM ** -0.5

LANES_V7X = 128
VMEM_LIMIT_BYTES_V7X = 56 * 1024 * 1024

TM_IN = 512
TM_FFN = 512
FF_CHUNK = 1024
TQ_WIN = 512
TR_NBR = 8
KV_ROWS_BLK = 4
ADA_TN = 1536
PIPE_LAGS = (2, 2)

NEG_INF = float("-inf")

A_HEAD_ORDER = (0, 3, 1, 4, 2, 5)


def _cparams(n_axes):
    return pltpu.CompilerParams(
        dimension_semantics=("arbitrary",) * n_axes,
        vmem_limit_bytes=VMEM_LIMIT_BYTES_V7X,
    )


def _lane_is_low(shape):
    lane = lax.broadcasted_iota(jnp.int32, shape, len(shape) - 1)
    return (lane % LANES_V7X) < HEAD_DIM


def _ada_kernel(c_ref, w_ref, b_ref, o_ref):
    c = c_ref[...]
    s = (c * jax.nn.sigmoid(c)).astype(jnp.bfloat16)
    w = w_ref[...].astype(jnp.bfloat16)
    o_ref[...] = jnp.dot(s, w, preferred_element_type=jnp.float32) + b_ref[...]


def _ada_call(c8, w_ada, b_ada):
    depth = w_ada.shape[0]
    n = w_ada.shape[2]
    return pl.pallas_call(
        _ada_kernel,
        grid=(depth, n // ADA_TN),
        in_specs=[
            pl.BlockSpec((8, D_MODEL), lambda l, j: (0, 0)),
            pl.BlockSpec((None, D_MODEL, ADA_TN), lambda l, j: (l, 0, j)),
            pl.BlockSpec((None, 1, ADA_TN), lambda l, j: (l, 0, j)),
        ],
        out_specs=pl.BlockSpec((None, 8, ADA_TN), lambda l, j: (l, 0, j)),
        out_shape=jax.ShapeDtypeStruct((depth, 8, n), jnp.float32),
        compiler_params=_cparams(2),
        name="ada_mod",
    )(c8, w_ada, b_ada.reshape(depth, 1, n))


def _head_rms(x, seg_ones, gain):
    sq = (x * x).astype(jnp.bfloat16)
    ssum = jnp.dot(sq, seg_ones, preferred_element_type=jnp.float32)
    return (x * lax.rsqrt(ssum * (1.0 / HEAD_DIM) + EPS)) * gain


def _rope(x, cos_t, sin_up, sin_dn):
    n = x.shape[1] // LANES_V7X
    outs = []
    for j in range(n):
        xj = x[:, j * LANES_V7X:(j + 1) * LANES_V7X]
        up = pltpu.roll(xj, LANES_V7X - ROPE_DIMS // 2, 1)
        dn = pltpu.roll(xj, ROPE_DIMS // 2, 1)
        outs.append(xj * cos_t + up * sin_up + dn * sin_dn)
    return outs[0] if n == 1 else jnp.concatenate(outs, axis=1)


def _inproj_kernel(x_ref, mod_ref, g1_ref, w_ref, seg_ref, gqa_ref, gka_ref, gqc_ref, gkc_ref,
                   gbv_ref, cos_ref, sup_ref, sdn_ref, ws_ref, bs_ref,
                   aq_ref, ak_ref, av_ref, ob_ref, cq_ref, ck_ref, cv_ref):
    x = x_ref[...]
    shift1 = mod_ref[0:1, :]
    scale1 = mod_ref[1:2, :]
    ms = jnp.mean(x * x, axis=-1, keepdims=True)
    h = ((x * lax.rsqrt(ms + EPS)) * g1_ref[...]) * (1.0 + scale1) + shift1
    proj = jnp.dot(h.astype(jnp.bfloat16), w_ref[...], preferred_element_type=jnp.float32)

    seg = seg_ref[...]
    cos_t, sin_up, sin_dn = cos_ref[...], sup_ref[...], sdn_ref[...]
    o = 0
    aq = proj[:, o:o + A_Q]; o += A_Q
    ak = proj[:, o:o + A_KV]; o += A_KV
    av = proj[:, o:o + A_KV]; o += A_KV
    bu = proj[:, o:o + B_W]; o += B_W
    bv = proj[:, o:o + B_W]; o += B_W
    cq = proj[:, o:o + C_W]; o += C_W
    ck = proj[:, o:o + C_W]; o += C_W
    cv = proj[:, o:o + C_W]

    aq_n = jnp.concatenate(
        [_head_rms(aq[:, j * 128:(j + 1) * 128], seg, gqa_ref[...]) for j in range(A_Q // 128)], axis=1)
    aq_ref[...] = _rope(aq_n, cos_t, sin_up, sin_dn).astype(jnp.bfloat16)
    ak_ref[...] = _rope(_head_rms(ak, seg, gka_ref[...]), cos_t, sin_up, sin_dn).astype(jnp.bfloat16)
    av_ref[...] = av.astype(jnp.bfloat16)

    cq_ref[...] = jnp.concatenate(
        [_head_rms(cq[:, j * 128:(j + 1) * 128], seg, gqc_ref[...]) for j in range(C_W // 128)],
        axis=1).astype(jnp.bfloat16)
    ck_ref[...] = jnp.concatenate(
        [_head_rms(ck[:, j * 128:(j + 1) * 128], seg, gkc_ref[...]) for j in range(C_W // 128)],
        axis=1).astype(jnp.bfloat16)
    cv_ref[...] = cv.astype(jnp.bfloat16)

    u = jax.nn.gelu(bu)
    v = jax.nn.gelu(bv)
    vms = jnp.mean(v * v, axis=-1, keepdims=True)
    vr = ((v * lax.rsqrt(vms + EPS)) * gbv_ref[...]).astype(jnp.bfloat16)
    low = _lane_is_low((CHUNK, LANES_V7X))
    zero = jnp.zeros((), jnp.bfloat16)
    tm = x.shape[0]
    for c in range(tm // CHUNK):
        rows = slice(c * CHUNK, (c + 1) * CHUNK)
        sv_parts = []
        for j in range(B_W // LANES_V7X):
            vj = vr[rows, j * LANES_V7X:(j + 1) * LANES_V7X]
            sv = jnp.dot(ws_ref[2 * j], jnp.where(low, vj, zero), preferred_element_type=jnp.float32)
            sv += jnp.dot(ws_ref[2 * j + 1], jnp.where(low, zero, vj), preferred_element_type=jnp.float32)
            sv_parts.append(sv)
        sv = jnp.concatenate(sv_parts, axis=1) + bs_ref[...]
        ob_ref[rows, :] = (u[rows, :] * sv).astype(jnp.bfloat16)


def _inproj_call(x, mod, layer, boff, p):
    bsz, seq, _ = x.shape
    tm = TM_IN
    tok = lambda w: pl.BlockSpec((None, tm, w), lambda b, i: (b, i, 0))
    const2 = lambda a: pl.BlockSpec(a.shape, lambda b, i: (0, 0))
    bf = lambda w: jax.ShapeDtypeStruct((bsz, seq, w), jnp.bfloat16)
    rope_spec = pl.BlockSpec((tm, LANES_V7X), lambda b, i: (i, 0))
    return pl.pallas_call(
        _inproj_kernel,
        grid=(bsz, seq // tm),
        in_specs=[
            tok(D_MODEL),
            pl.BlockSpec((None, None, 6, D_MODEL), lambda b, i: (layer, boff + b, 0, 0)),
            const2(p["g1"]), const2(p["w_in"]), const2(p["seg"]),
            const2(p["gqa"]), const2(p["gka"]), const2(p["gqc"]), const2(p["gkc"]), const2(p["gbv"]),
            rope_spec, rope_spec, rope_spec,
            pl.BlockSpec(p["ws"].shape, lambda b, i: (0, 0, 0)),
            const2(p["bs"]),
        ],
        out_specs=[tok(A_Q), tok(A_KV), tok(A_KV), tok(B_W), tok(C_W), tok(C_W), tok(C_W)],
        out_shape=[bf(A_Q), bf(A_KV), bf(A_KV), bf(B_W), bf(C_W), bf(C_W), bf(C_W)],
        compiler_params=_cparams(2),
        name="inproj",
    )(x, mod, p["g1"], p["w_in"], p["seg"], p["gqa"], p["gka"], p["gqc"], p["gkc"], p["gbv"],
      p["cos"], p["sin_up"], p["sin_dn"], p["ws"], p["bs"])


_NT_DIMS = (((1,), (1,)), ((), ()))


def _qk_two_heads(q, k, low):
    zero = jnp.zeros((), q.dtype)
    s_lo = lax.dot_general(jnp.where(low, q, zero), k, _NT_DIMS, preferred_element_type=jnp.float32)
    s_hi = lax.dot_general(jnp.where(low, zero, q), k, _NT_DIMS, preferred_element_type=jnp.float32)
    return s_lo, s_hi


def _softmax_numerator(s, extra_logit=None):
    m = jnp.max(s, axis=-1, keepdims=True)
    if extra_logit is not None:
        m = jnp.maximum(m, extra_logit)
    e = jnp.exp(s - m)
    l = jnp.sum(e, axis=-1, keepdims=True)
    if extra_logit is not None:
        l = l + jnp.exp(extra_logit - m)
    return e.astype(jnp.bfloat16), l


def _pv_two_heads(e_lo, l_lo, e_hi, l_hi, v, low):
    o_lo = jnp.dot(e_lo, v, preferred_element_type=jnp.float32) / l_lo
    o_hi = jnp.dot(e_hi, v, preferred_element_type=jnp.float32) / l_hi
    return jnp.where(low, o_lo, o_hi).astype(jnp.bfloat16)


def _software_pipeline(n, stages, lags):
    offs = [0]
    for lag in lags:
        offs.append(offs[-1] + lag)
    vals = [dict() for _ in stages]
    for step in range(n + offs[-1]):
        for si, fn in enumerate(stages):
            u = step - offs[si]
            if 0 <= u < n:
                vals[si][u] = fn(u) if si == 0 else fn(u, vals[si - 1].pop(u))


def _window_kernel(sink_ref, band_ref, q_ref, kp_ref, kc_ref, kn_ref, vp_ref, vc_ref, vn_ref, o_ref,
                   kwin, vwin):
    i = pl.program_id(1)
    last = pl.num_programs(1) - 1
    kwin[0:BLOCK, :] = kp_ref[...]
    kwin[BLOCK:BLOCK + TQ_WIN, :] = kc_ref[...]
    kwin[BLOCK + TQ_WIN:, :] = kn_ref[...]
    vwin[0:BLOCK, :] = vp_ref[...]
    vwin[BLOCK:BLOCK + TQ_WIN, :] = vc_ref[...]
    vwin[BLOCK + TQ_WIN:, :] = vn_ref[...]
    band = band_ref[...]
    col = lax.broadcasted_iota(jnp.int32, band.shape, 1)
    low = _lane_is_low((BLOCK, LANES_V7X))
    nt = TQ_WIN // BLOCK
    ng = A_Q // LANES_V7X
    mask_first = band + jnp.where(col < BLOCK, jnp.where(i == 0, NEG_INF, 0.0), 0.0)
    mask_last = band + jnp.where(col >= 2 * BLOCK, jnp.where(i == last, NEG_INF, 0.0), 0.0)

    def unit(u):
        t, g = divmod(u, ng)
        return t, g, slice(t * BLOCK, (t + 1) * BLOCK), slice(g * LANES_V7X, (g + 1) * LANES_V7X)

    def qk(u):
        t, g, rows, cols = unit(u)
        return _qk_two_heads(q_ref[rows, cols], kwin[t * BLOCK:(t + 3) * BLOCK, :], low)

    def sm(u, s):
        t, g, rows, cols = unit(u)
        mask = mask_first if t == 0 else (mask_last if t == nt - 1 else band)
        return (_softmax_numerator(s[0] + mask, sink_ref[2 * g])
                + _softmax_numerator(s[1] + mask, sink_ref[2 * g + 1]))

    def pv(u, e):
        t, g, rows, cols = unit(u)
        o_ref[rows, cols] = _pv_two_heads(*e, vwin[t * BLOCK:(t + 3) * BLOCK, :], low)

    _software_pipeline(nt * ng, (qk, sm, pv), PIPE_LAGS)


def _window_call(aq, ak, av, sink, band):
    bsz, seq, _ = aq.shape
    nt = TQ_WIN // BLOCK
    nb = seq // BLOCK
    assert nt >= 2
    cur = lambda w: pl.BlockSpec((None, TQ_WIN, w), lambda b, i: (b, i, 0))
    prev = pl.BlockSpec((None, BLOCK, A_KV), lambda b, i: (b, jnp.maximum(i * nt - 1, 0), 0))
    nxt = pl.BlockSpec((None, BLOCK, A_KV), lambda b, i: (b, jnp.minimum(i * nt + nt, nb - 1), 0))
    win = pltpu.VMEM((TQ_WIN + 2 * BLOCK, A_KV), jnp.bfloat16)
    return pl.pallas_call(
        _window_kernel,
        grid=(bsz, seq // TQ_WIN),
        in_specs=[
            pl.BlockSpec(memory_space=pltpu.SMEM),
            pl.BlockSpec(band.shape, lambda b, i: (0, 0)),
            cur(A_Q), prev, cur(A_KV), nxt, prev, cur(A_KV), nxt,
        ],
        out_specs=cur(A_Q),
        out_shape=jax.ShapeDtypeStruct((bsz, seq, A_Q), jnp.bfloat16),
        scratch_shapes=[win, win],
        compiler_params=_cparams(2),
        name="window_attn",
    )(sink, band, aq, ak, ak, ak, av, av, av)


def _nbr_kernel(rows, bias_ref, q_ref, k0, k1, k2, k3, v0, v1, v2, v3, o_ref, kwin, vwin):
    i = pl.program_id(1)
    blk = KV_ROWS_BLK * GRID_W
    for n, (kr, vr) in enumerate(((k0, v0), (k1, v1), (k2, v2), (k3, v3))):
        kwin[n * blk:(n + 1) * blk, :] = kr[...]
        vwin[n * blk:(n + 1) * blk, :] = vr[...]
    r0 = i * TR_NBR
    win_start = _nbr_window_block(i, rows) * KV_ROWS_BLK
    low = _lane_is_low((GRID_W, LANES_V7X))
    nkeys = NA_KH * GRID_W
    ng = C_W // LANES_V7X
    key_off, bias_base = [], []
    for t in range(TR_NBR):
        r = r0 + t
        rs = jnp.clip(r - NA_KH // 2, 0, rows - NA_KH)
        key_off.append(pl.multiple_of((rs - win_start) * GRID_W, GRID_W))
        bias_base.append(NA_KH - 1 - (r - rs))

    def unit(u):
        t, g = divmod(u, ng)
        return t, g, slice(t * GRID_W, (t + 1) * GRID_W), slice(g * LANES_V7X, (g + 1) * LANES_V7X)

    def qk(u):
        t, g, qrows, cols = unit(u)
        return _qk_two_heads(q_ref[qrows, cols], kwin[pl.ds(key_off[t], nkeys), cols], low)

    def bias(h, t):
        return jnp.concatenate([bias_ref[h, bias_base[t] + 2 * m] for m in range(NA_KH // 2)], axis=1)

    def sm(u, s):
        t, g, qrows, cols = unit(u)
        return (_softmax_numerator(s[0] + bias(2 * g, t)) + _softmax_numerator(s[1] + bias(2 * g + 1, t)))

    def pv(u, e):
        t, g, qrows, cols = unit(u)
        o_ref[qrows, cols] = _pv_two_heads(*e, vwin[pl.ds(key_off[t], nkeys), cols], low)

    _software_pipeline(TR_NBR * ng, (qk, sm, pv), PIPE_LAGS)


def _nbr_window_block(i, rows):
    per_tile = TR_NBR // KV_ROWS_BLK
    nblk = rows // KV_ROWS_BLK
    return jnp.clip(i * per_tile - 1, 0, nblk - (per_tile + 2))


def _nbr_call(cq, ck, cv, bias_tab):
    bsz, seq, _ = cq.shape
    rows = seq // GRID_W
    tq = TR_NBR * GRID_W
    blk = KV_ROWS_BLK * GRID_W
    nwin = TR_NBR // KV_ROWS_BLK + 2
    assert nwin == 4 and rows % TR_NBR == 0 and rows >= nwin * KV_ROWS_BLK and NA_KH // 2 <= KV_ROWS_BLK
    kv_spec = lambda n: pl.BlockSpec((None, blk, C_W), lambda b, i: (b, _nbr_window_block(i, rows) + n, 0))
    q_spec = pl.BlockSpec((None, tq, C_W), lambda b, i: (b, i, 0))
    kv_specs = [kv_spec(n) for n in range(nwin)]
    return pl.pallas_call(
        partial(_nbr_kernel, rows),
        grid=(bsz, rows // TR_NBR),
        in_specs=[pl.BlockSpec(bias_tab.shape, lambda b, i: (0, 0, 0, 0)), q_spec] + kv_specs + kv_specs,
        out_specs=q_spec,
        out_shape=jax.ShapeDtypeStruct((bsz, seq, C_W), jnp.bfloat16),
        scratch_shapes=[pltpu.VMEM((nwin * blk, C_W), jnp.bfloat16),
                        pltpu.VMEM((nwin * blk, C_W), jnp.bfloat16)],
        compiler_params=_cparams(2),
        name="nbr_attn",
    )(bias_tab, cq, ck, ck, ck, ck, cv, cv, cv, cv)


def _ffn_kernel(x_ref, oa_ref, ob_ref, oc_ref, mod_ref, g2_ref, wo_ref, w1_ref, w2_ref, y_ref):
    gate1 = mod_ref[2:3, :]
    shift2 = mod_ref[3:4, :]
    scale2 = mod_ref[4:5, :]
    gate2 = mod_ref[5:6, :]
    mix_in = jnp.concatenate([oa_ref[...], ob_ref[...], oc_ref[...]], axis=1)
    mix = jnp.dot(mix_in, wo_ref[...], preferred_element_type=jnp.float32)
    x1 = x_ref[...] + gate1 * mix
    ms = jnp.mean(x1 * x1, axis=-1, keepdims=True)
    h2 = (((x1 * lax.rsqrt(ms + EPS)) * g2_ref[...]) * (1.0 + scale2) + shift2).astype(jnp.bfloat16)
    ff = None
    for j in range(D_FF // FF_CHUNK):
        cols = slice(j * FF_CHUNK, (j + 1) * FF_CHUNK)
        a = jnp.maximum(jnp.dot(h2, w1_ref[:, cols], preferred_element_type=jnp.float32), 0.0)
        part = jnp.dot((a * a).astype(jnp.bfloat16), w2_ref[cols, :], preferred_element_type=jnp.float32)
        ff = part if ff is None else ff + part
    y_ref[...] = x1 + gate2 * ff


def _ffn_call(x, oa, ob, oc, mod, layer, boff, p):
    bsz, seq, _ = x.shape
    tm = TM_FFN
    tok = lambda w: pl.BlockSpec((None, tm, w), lambda b, i: (b, i, 0))
    const2 = lambda a: pl.BlockSpec(a.shape, lambda b, i: (0, 0), pipeline_mode=pl.Buffered(1))
    return pl.pallas_call(
        _ffn_kernel,
        grid=(bsz, seq // tm),
        in_specs=[
            tok(D_MODEL), tok(A_Q), tok(B_W), tok(C_W),
            pl.BlockSpec((None, None, 6, D_MODEL), lambda b, i: (layer, boff + b, 0, 0)),
            const2(p["g2"]), const2(p["w_out"]), const2(p["w_ff1"]), const2(p["w_ff2"]),
        ],
        out_specs=tok(D_MODEL),
        out_shape=jax.ShapeDtypeStruct((bsz, seq, D_MODEL), jnp.float32),
        compiler_params=_cparams(2),
        name="outproj_ffn",
    )(x, oa, ob, oc, mod, p["g2"], p["w_out"], p["w_ff1"], p["w_ff2"])


def _tile_heads(g, scale=1.0):
    return (jnp.tile(g.astype(jnp.float32), LANES_V7X // HEAD_DIM) * scale).reshape(1, LANES_V7X)


def _rope_tables(seq):
    half = ROPE_DIMS // 2
    inv = ROPE_THETA ** (-jnp.arange(0, ROPE_DIMS, 2, dtype=jnp.float32) / ROPE_DIMS)
    ang = jnp.arange(seq, dtype=jnp.float32)[:, None] * inv[None, :]
    cos, sin = jnp.cos(ang), jnp.sin(ang)
    one = jnp.ones((seq, HEAD_DIM - ROPE_DIMS), jnp.float32)
    zero_t = jnp.zeros((seq, HEAD_DIM - ROPE_DIMS), jnp.float32)
    zero_h = jnp.zeros((seq, half), jnp.float32)
    rep = LANES_V7X // HEAD_DIM
    cos_t = jnp.tile(jnp.concatenate([cos, cos, one], axis=1), (1, rep))
    sin_up = jnp.tile(jnp.concatenate([-sin, zero_h, zero_t], axis=1), (1, rep))
    sin_dn = jnp.tile(jnp.concatenate([zero_h, sin, zero_t], axis=1), (1, rep))
    return cos_t, sin_up, sin_dn


def _band_mask():
    qi = np.arange(BLOCK)[:, None]
    kj = np.arange(3 * BLOCK)[None, :]
    ok = np.abs(kj - BLOCK - qi) <= BLOCK
    return jnp.asarray(np.where(ok, 0.0, NEG_INF).astype(np.float32))


def _nbr_bias_table(rel_bias):
    col = np.arange(GRID_W)
    cs = np.clip(col - NA_KW // 2, 0, GRID_W - NA_KW)
    col_in = (col[None, :] >= cs[:, None]) & (col[None, :] < cs[:, None] + NA_KW)
    dc = np.clip(col[None, :] - col[:, None] + NA_KW - 1, 0, 2 * NA_KW - 2)
    ndc = 2 * NA_KW - 1
    onehot = (dc.reshape(-1)[None, :] == np.arange(ndc)[:, None]).astype(np.float32)
    tm = jnp.einsum("hrd,dx->hrx", rel_bias.astype(jnp.float32), jnp.asarray(onehot),
                    precision=lax.Precision.HIGHEST)
    tm = jnp.where(jnp.asarray(col_in.reshape(-1))[None, None, :], tm, NEG_INF)
    tm = tm.reshape(C_HEADS, 2 * NA_KH - 1, GRID_W, GRID_W)
    return jnp.concatenate([tm[:, :-1], tm[:, 1:]], axis=-1)


def _layer_params(l, seq_tables, norm1_g, norm2_g, w_in, qa_norm_g, ka_norm_g, a_sink, bv_norm_g,
                  w_spatial, b_spatial, qc_norm_g, kc_norm_g, c_rel_bias, w_out, w_ff1, w_ff2):
    order = np.asarray(A_HEAD_ORDER)
    q_cols = (order[:, None] * HEAD_DIM + np.arange(HEAD_DIM)[None, :]).reshape(-1)
    in_cols = np.concatenate([q_cols, np.arange(A_Q, D_IN)])
    out_rows = np.concatenate([q_cols, np.arange(A_Q, D_MIX)])
    seg = np.kron(np.eye(LANES_V7X // HEAD_DIM), np.ones((HEAD_DIM, HEAD_DIM))).astype(np.float32)
    cos_t, sin_up, sin_dn = seq_tables
    return {
        "g1": norm1_g[l].reshape(1, D_MODEL),
        "g2": norm2_g[l].reshape(1, D_MODEL),
        "w_in": w_in[l][:, in_cols].astype(jnp.bfloat16),
        "w_out": w_out[l][out_rows, :].astype(jnp.bfloat16),
        "w_ff1": w_ff1[l].astype(jnp.bfloat16),
        "w_ff2": w_ff2[l].astype(jnp.bfloat16),
        "seg": jnp.asarray(seg, jnp.bfloat16),
        "gqa": _tile_heads(qa_norm_g[l], ATTN_SCALE),
        "gka": _tile_heads(ka_norm_g[l]),
        "gqc": _tile_heads(qc_norm_g[l], ATTN_SCALE),
        "gkc": _tile_heads(kc_norm_g[l]),
        "gbv": bv_norm_g[l].reshape(1, B_W),
        "cos": cos_t, "sin_up": sin_up, "sin_dn": sin_dn,
        "ws": w_spatial[l].astype(jnp.bfloat16),
        "bs": jnp.repeat(b_spatial[l].T, HEAD_DIM, axis=1),
        "sink": a_sink[l][order].astype(jnp.float32),
        "nbr_bias": _nbr_bias_table(c_rel_bias[l]),
    }


def _trunk(x, mod, boff, layer_params):
    for l, p in enumerate(layer_params):
        aq, ak, av, ob, cq, ck, cv = _inproj_call(x, mod, l, boff, p)
        oa = _window_call(aq, ak, av, p["sink"], p["band"])
        oc = _nbr_call(cq, ck, cv, p["nbr_bias"])
        x = _ffn_call(x, oa, ob, oc, mod, l, boff, p)
    return x


def kernel(x_prompt, x_sample, c_prompt, c_sample, norm1_g, norm2_g, w_ada, b_ada, w_in, qa_norm_g, ka_norm_g, a_sink, bv_norm_g, w_spatial, b_spatial, qc_norm_g, kc_norm_g, c_rel_bias, w_out, w_ff1, w_ff2):
    depth = w_in.shape[0]
    nb_prompt, seq_prompt, _ = x_prompt.shape
    nb_sample, seq_sample, _ = x_sample.shape
    assert nb_prompt + nb_sample <= 8
    for seq in (seq_prompt, seq_sample):
        assert seq % TM_IN == 0 and seq % TM_FFN == 0 and seq % TQ_WIN == 0 and seq // BLOCK >= 3

    c8 = jnp.zeros((8, D_MODEL), jnp.float32)
    c8 = c8.at[:nb_prompt].set(c_prompt).at[nb_prompt:nb_prompt + nb_sample].set(c_sample)
    mod = _ada_call(c8, w_ada, b_ada).reshape(depth, 8, 6, D_MODEL)

    seq_tables = _rope_tables(max(seq_prompt, seq_sample))
    band = _band_mask()
    layer_params = []
    for l in range(depth):
        p = _layer_params(l, seq_tables, norm1_g, norm2_g, w_in, qa_norm_g, ka_norm_g, a_sink,
                          bv_norm_g, w_spatial, b_spatial, qc_norm_g, kc_norm_g, c_rel_bias,
                          w_out, w_ff1, w_ff2)
        p["band"] = band
        layer_params.append(p)

    y_prompt = _trunk(x_prompt, mod, 0, layer_params)
    y_sample = _trunk(x_sample, mod, nb_prompt, layer_params)
    return (y_prompt, y_sample)
```

```python
from functools import partial

import numpy as np
import jax
import jax.numpy as jnp
from jax import lax
from jax.experimental import pallas as pl
from jax.experimental.pallas import tpu as pltpu

D_MODEL = 1024
HEAD_DIM = 64
A_HEADS = 6
A_KV_HEADS = 2
A_GROUP = A_HEADS // A_KV_HEADS
B_GROUPS = 4
C_HEADS = 6
A_Q = A_HEADS * HEAD_DIM
A_KV = A_KV_HEADS * HEAD_DIM
B_W = B_GROUPS * HEAD_DIM
C_W = C_HEADS * HEAD_DIM
D_IN = A_Q + 2 * A_KV + 2 * B_W + 3 * C_W
D_MIX = A_Q + B_W + C_W
BLOCK = 128
CHUNK = 128
GRID_W = 64
NA_KH = 8
NA_KW = 16
ROPE_THETA = 500000.0
ROPE_DIMS = HEAD_DIM // 4
D_FF = 4 * D_MODEL
EPS = 1e-6
ATTN_SCALE = HEAD_DIM ** -0.5

LANES_V7X = 128
MXU_DIM_V7X = 256
SEG_W = MXU_DIM_V7X
VMEM_LIMIT_BYTES_V7X = 56 * 1024 * 1024

TM_IN = 512
TM_FFN = 512
FF_CHUNK = 1024
TQ_WIN = 512
TR_NBR = 8
KV_ROWS_BLK = 4
ADA_TN = 1536
PIPE_LAGS = (2, 2)

NEG_INF = float("-inf")

A_HEAD_ORDER = (0, 3, 1, 4, 2, 5)


def _cparams(n_axes):
    return pltpu.CompilerParams(
        dimension_semantics=("arbitrary",) * n_axes,
        vmem_limit_bytes=VMEM_LIMIT_BYTES_V7X,
    )


def _lane_is_low(shape):
    lane = lax.broadcasted_iota(jnp.int32, shape, len(shape) - 1)
    return (lane % LANES_V7X) < HEAD_DIM


def _ada_kernel(c_ref, w_ref, b_ref, o_ref):
    c = c_ref[...]
    s = (c * jax.nn.sigmoid(c)).astype(jnp.bfloat16)
    w = w_ref[...].astype(jnp.bfloat16)
    o_ref[...] = jnp.dot(s, w, preferred_element_type=jnp.float32) + b_ref[...]


def _ada_call(c8, w_ada, b_ada):
    depth = w_ada.shape[0]
    n = w_ada.shape[2]
    return pl.pallas_call(
        _ada_kernel,
        grid=(depth, n // ADA_TN),
        in_specs=[
            pl.BlockSpec((8, D_MODEL), lambda l, j: (0, 0)),
            pl.BlockSpec((None, D_MODEL, ADA_TN), lambda l, j: (l, 0, j)),
            pl.BlockSpec((None, 1, ADA_TN), lambda l, j: (l, 0, j)),
        ],
        out_specs=pl.BlockSpec((None, 8, ADA_TN), lambda l, j: (l, 0, j)),
        out_shape=jax.ShapeDtypeStruct((depth, 8, n), jnp.float32),
        compiler_params=_cparams(2),
        name="ada_mod",
    )(c8, w_ada, b_ada.reshape(depth, 1, n))


def _head_rms(x, seg_ones, gain):
    outs = []
    for j in range(x.shape[1] // SEG_W):
        xj = x[:, j * SEG_W:(j + 1) * SEG_W]
        ssum = jnp.dot((xj * xj).astype(jnp.bfloat16), seg_ones, preferred_element_type=jnp.float32)
        outs.append((xj * lax.rsqrt(ssum * (1.0 / HEAD_DIM) + EPS)) * gain[:, j * SEG_W:(j + 1) * SEG_W])
    return jnp.concatenate(outs, axis=1)


def _rope(x, cos_t, sin_up, sin_dn):
    n = x.shape[1] // LANES_V7X
    outs = []
    for j in range(n):
        xj = x[:, j * LANES_V7X:(j + 1) * LANES_V7X]
        up = pltpu.roll(xj, LANES_V7X - ROPE_DIMS // 2, 1)
        dn = pltpu.roll(xj, ROPE_DIMS // 2, 1)
        outs.append(xj * cos_t + up * sin_up + dn * sin_dn)
    return outs[0] if n == 1 else jnp.concatenate(outs, axis=1)


def _inproj_kernel(x_ref, mod_ref, g1_ref, w_ref, seg_ref, ga_ref, gc_ref,
                   gbv_ref, cos_ref, sup_ref, sdn_ref, ws_ref, bs_ref,
                   aq_ref, ak_ref, av_ref, ob_ref, cq_ref, ck_ref, cv_ref):
    x = x_ref[...]
    shift1 = mod_ref[0:1, :]
    scale1 = mod_ref[1:2, :]
    ms = jnp.mean(x * x, axis=-1, keepdims=True)
    h = ((x * lax.rsqrt(ms + EPS)) * g1_ref[...]) * (1.0 + scale1) + shift1
    proj = jnp.dot(h.astype(jnp.bfloat16), w_ref[...], preferred_element_type=jnp.float32)

    seg = seg_ref[...]
    o_av = A_Q + A_KV
    o_bu = o_av + A_KV
    o_bv = o_bu + B_W
    o_cq = o_bv + B_W
    o_cv = o_cq + 2 * C_W

    a_n = _rope(_head_rms(proj[:, 0:o_av], seg, ga_ref[...]), cos_ref[...], sup_ref[...], sdn_ref[...])
    aq_ref[...] = a_n[:, :A_Q].astype(jnp.bfloat16)
    ak_ref[...] = a_n[:, A_Q:].astype(jnp.bfloat16)
    av_ref[...] = proj[:, o_av:o_bu].astype(jnp.bfloat16)

    c_n = _head_rms(proj[:, o_cq:o_cv], seg, gc_ref[...])
    cq_ref[...] = c_n[:, :C_W].astype(jnp.bfloat16)
    ck_ref[...] = c_n[:, C_W:].astype(jnp.bfloat16)
    cv_ref[...] = proj[:, o_cv:].astype(jnp.bfloat16)

    u = jax.nn.gelu(proj[:, o_bu:o_bv])
    v = jax.nn.gelu(proj[:, o_bv:o_cq])
    vms = jnp.mean(v * v, axis=-1, keepdims=True)
    vr = ((v * lax.rsqrt(vms + EPS)) * gbv_ref[...]).astype(jnp.bfloat16)
    low = _lane_is_low((CHUNK, LANES_V7X))
    zero = jnp.zeros((), jnp.bfloat16)
    tm = x.shape[0]
    for c in range(tm // CHUNK):
        rows = slice(c * CHUNK, (c + 1) * CHUNK)
        sv_parts = []
        for j in range(B_W // LANES_V7X):
            vj = vr[rows, j * LANES_V7X:(j + 1) * LANES_V7X]
            v2 = jnp.concatenate([jnp.where(low, vj, zero), jnp.where(low, zero, vj)], axis=0)
            sv_parts.append(jnp.dot(ws_ref[j], v2, preferred_element_type=jnp.float32))
        sv = jnp.concatenate(sv_parts, axis=1) + bs_ref[...]
        ob_ref[rows, :] = (u[rows, :] * sv).astype(jnp.bfloat16)


def _inproj_call(x, mod, layer, boff, p):
    bsz, seq, _ = x.shape
    tm = TM_IN
    tok = lambda w: pl.BlockSpec((None, tm, w), lambda b, i: (b, i, 0))
    const2 = lambda a: pl.BlockSpec(a.shape, lambda b, i: (0, 0))
    bf = lambda w: jax.ShapeDtypeStruct((bsz, seq, w), jnp.bfloat16)
    rope_spec = pl.BlockSpec((tm, LANES_V7X), lambda b, i: (i, 0))
    return pl.pallas_call(
        _inproj_kernel,
        grid=(bsz, seq // tm),
        in_specs=[
            tok(D_MODEL),
            pl.BlockSpec((None, None, 6, D_MODEL), lambda b, i: (layer, boff + b, 0, 0)),
            const2(p["g1"]), const2(p["w_in"]), const2(p["seg"]),
            const2(p["ga"]), const2(p["gc"]), const2(p["gbv"]),
            rope_spec, rope_spec, rope_spec,
            pl.BlockSpec(p["ws"].shape, lambda b, i: (0, 0, 0)),
            const2(p["bs"]),
        ],
        out_specs=[tok(A_Q), tok(A_KV), tok(A_KV), tok(B_W), tok(C_W), tok(C_W), tok(C_W)],
        out_shape=[bf(A_Q), bf(A_KV), bf(A_KV), bf(B_W), bf(C_W), bf(C_W), bf(C_W)],
        compiler_params=_cparams(2),
        name="inproj",
    )(x, mod, p["g1"], p["w_in"], p["seg"], p["ga"], p["gc"], p["gbv"],
      p["cos"], p["sin_up"], p["sin_dn"], p["ws"], p["bs"])


_NT_DIMS = (((1,), (1,)), ((), ()))


def _qk_two_heads(q, k, low):
    zero = jnp.zeros((), q.dtype)
    q2 = jnp.concatenate([jnp.where(low, q, zero), jnp.where(low, zero, q)], axis=0)
    return lax.dot_general(q2, k, _NT_DIMS, preferred_element_type=jnp.float32)


def _softmax_numerator(s, extra_logit=None):
    m = jnp.max(s, axis=-1, keepdims=True)
    if extra_logit is not None:
        m = jnp.maximum(m, extra_logit)
    e = jnp.exp(s - m)
    l = jnp.sum(e, axis=-1, keepdims=True)
    if extra_logit is not None:
        l = l + jnp.exp(extra_logit - m)
    return e.astype(jnp.bfloat16), l


def _pv_two_heads(e2, l2, v, low):
    o2 = jnp.dot(e2, v, preferred_element_type=jnp.float32) / l2
    n = o2.shape[0] // 2
    return jnp.where(low, o2[:n], o2[n:]).astype(jnp.bfloat16)


def _software_pipeline(n, stages, lags):
    offs = [0]
    for lag in lags:
        offs.append(offs[-1] + lag)
    vals = [dict() for _ in stages]
    for step in range(n + offs[-1]):
        for si, fn in enumerate(stages):
            u = step - offs[si]
            if 0 <= u < n:
                vals[si][u] = fn(u) if si == 0 else fn(u, vals[si - 1].pop(u))


def _window_kernel(sink_ref, band_ref, q_ref, kp_ref, kc_ref, kn_ref, vp_ref, vc_ref, vn_ref, o_ref,
                   kwin, vwin):
    i = pl.program_id(1)
    last = pl.num_programs(1) - 1
    kwin[0:BLOCK, :] = kp_ref[...]
    kwin[BLOCK:BLOCK + TQ_WIN, :] = kc_ref[...]
    kwin[BLOCK + TQ_WIN:, :] = kn_ref[...]
    vwin[0:BLOCK, :] = vp_ref[...]
    vwin[BLOCK:BLOCK + TQ_WIN, :] = vc_ref[...]
    vwin[BLOCK + TQ_WIN:, :] = vn_ref[...]
    band = band_ref[...]
    col = lax.broadcasted_iota(jnp.int32, band.shape, 1)
    is_low_head = lax.broadcasted_iota(jnp.int32, (2 * BLOCK, 1), 0) < BLOCK
    low = _lane_is_low((BLOCK, LANES_V7X))
    nt = TQ_WIN // BLOCK
    ng = A_Q // LANES_V7X
    mask_first = band + jnp.where(col < BLOCK, jnp.where(i == 0, NEG_INF, 0.0), 0.0)
    mask_last = band + jnp.where(col >= 2 * BLOCK, jnp.where(i == last, NEG_INF, 0.0), 0.0)

    def unit(u):
        t, g = divmod(u, ng)
        return t, g, slice(t * BLOCK, (t + 1) * BLOCK), slice(g * LANES_V7X, (g + 1) * LANES_V7X)

    def qk(u):
        t, g, rows, cols = unit(u)
        return _qk_two_heads(q_ref[rows, cols], kwin[t * BLOCK:(t + 3) * BLOCK, :], low)

    def sm(u, s):
        t, g, rows, cols = unit(u)
        mask = mask_first if t == 0 else (mask_last if t == nt - 1 else band)
        sink = jnp.where(is_low_head, sink_ref[2 * g], sink_ref[2 * g + 1])
        return _softmax_numerator(s + mask, sink)

    def pv(u, e):
        t, g, rows, cols = unit(u)
        o_ref[rows, cols] = _pv_two_heads(*e, vwin[t * BLOCK:(t + 3) * BLOCK, :], low)

    _software_pipeline(nt * ng, (qk, sm, pv), PIPE_LAGS)


def _window_call(aq, ak, av, sink, band):
    bsz, seq, _ = aq.shape
    nt = TQ_WIN // BLOCK
    nb = seq // BLOCK
    assert nt >= 2
    cur = lambda w: pl.BlockSpec((None, TQ_WIN, w), lambda b, i: (b, i, 0))
    prev = pl.BlockSpec((None, BLOCK, A_KV), lambda b, i: (b, jnp.maximum(i * nt - 1, 0), 0))
    nxt = pl.BlockSpec((None, BLOCK, A_KV), lambda b, i: (b, jnp.minimum(i * nt + nt, nb - 1), 0))
    win = pltpu.VMEM((TQ_WIN + 2 * BLOCK, A_KV), jnp.bfloat16)
    return pl.pallas_call(
        _window_kernel,
        grid=(bsz, seq // TQ_WIN),
        in_specs=[
            pl.BlockSpec(memory_space=pltpu.SMEM),
            pl.BlockSpec(band.shape, lambda b, i: (0, 0)),
            cur(A_Q), prev, cur(A_KV), nxt, prev, cur(A_KV), nxt,
        ],
        out_specs=cur(A_Q),
        out_shape=jax.ShapeDtypeStruct((bsz, seq, A_Q), jnp.bfloat16),
        scratch_shapes=[win, win],
        compiler_params=_cparams(2),
        name="window_attn",
    )(sink, band, aq, ak, ak, ak, av, av, av)


def _nbr_kernel(rows, bias_ref, q_ref, k0, k1, k2, k3, v0, v1, v2, v3, o_ref, kwin, vwin):
    i = pl.program_id(1)
    blk = KV_ROWS_BLK * GRID_W
    for n, (kr, vr) in enumerate(((k0, v0), (k1, v1), (k2, v2), (k3, v3))):
        kwin[n * blk:(n + 1) * blk, :] = kr[...]
        vwin[n * blk:(n + 1) * blk, :] = vr[...]
    r0 = i * TR_NBR
    win_start = _nbr_window_block(i, rows) * KV_ROWS_BLK
    low = _lane_is_low((GRID_W, LANES_V7X))
    nkeys = NA_KH * GRID_W
    ng = C_W // LANES_V7X
    key_off, bias_base = [], []
    for t in range(TR_NBR):
        r = r0 + t
        rs = jnp.clip(r - NA_KH // 2, 0, rows - NA_KH)
        key_off.append(pl.multiple_of((rs - win_start) * GRID_W, GRID_W))
        bias_base.append(NA_KH - 1 - (r - rs))

    def unit(u):
        t, g = divmod(u, ng)
        return t, g, slice(t * GRID_W, (t + 1) * GRID_W), slice(g * LANES_V7X, (g + 1) * LANES_V7X)

    def qk(u):
        t, g, qrows, cols = unit(u)
        return _qk_two_heads(q_ref[qrows, cols], kwin[pl.ds(key_off[t], nkeys), cols], low)

    def sm(u, s):
        t, g, qrows, cols = unit(u)
        bias = jnp.concatenate([bias_ref[g, bias_base[t] + 2 * m] for m in range(NA_KH // 2)], axis=1)
        return _softmax_numerator(s + bias)

    def pv(u, e):
        t, g, qrows, cols = unit(u)
        o_ref[qrows, cols] = _pv_two_heads(*e, vwin[pl.ds(key_off[t], nkeys), cols], low)

    _software_pipeline(TR_NBR * ng, (qk, sm, pv), PIPE_LAGS)


def _nbr_window_block(i, rows):
    per_tile = TR_NBR // KV_ROWS_BLK
    nblk = rows // KV_ROWS_BLK
    return jnp.clip(i * per_tile - 1, 0, nblk - (per_tile + 2))


def _nbr_call(cq, ck, cv, bias_tab):
    bsz, seq, _ = cq.shape
    rows = seq // GRID_W
    tq = TR_NBR * GRID_W
    blk = KV_ROWS_BLK * GRID_W
    nwin = TR_NBR // KV_ROWS_BLK + 2
    assert nwin == 4 and rows % TR_NBR == 0 and rows >= nwin * KV_ROWS_BLK and NA_KH // 2 <= KV_ROWS_BLK
    kv_spec = lambda n: pl.BlockSpec((None, blk, C_W), lambda b, i: (b, _nbr_window_block(i, rows) + n, 0))
    q_spec = pl.BlockSpec((None, tq, C_W), lambda b, i: (b, i, 0))
    kv_specs = [kv_spec(n) for n in range(nwin)]
    return pl.pallas_call(
        partial(_nbr_kernel, rows),
        grid=(bsz, rows // TR_NBR),
        in_specs=[pl.BlockSpec(bias_tab.shape, lambda b, i: (0, 0, 0, 0)), q_spec] + kv_specs + kv_specs,
        out_specs=q_spec,
        out_shape=jax.ShapeDtypeStruct((bsz, seq, C_W), jnp.bfloat16),
        scratch_shapes=[pltpu.VMEM((nwin * blk, C_W), jnp.bfloat16),
                        pltpu.VMEM((nwin * blk, C_W), jnp.bfloat16)],
        compiler_params=_cparams(2),
        name="nbr_attn",
    )(bias_tab, cq, ck, ck, ck, ck, cv, cv, cv, cv)


def _ffn_kernel(x_ref, oa_ref, ob_ref, oc_ref, mod_ref, g2_ref, wo_ref, w1_ref, w2_ref, y_ref):
    gate1 = mod_ref[2:3, :]
    shift2 = mod_ref[3:4, :]
    scale2 = mod_ref[4:5, :]
    gate2 = mod_ref[5:6, :]
    mix_in = jnp.concatenate([oa_ref[...], ob_ref[...], oc_ref[...]], axis=1)
    mix = jnp.dot(mix_in, wo_ref[...], preferred_element_type=jnp.float32)
    x1 = x_ref[...] + gate1 * mix
    ms = jnp.mean(x1 * x1, axis=-1, keepdims=True)
    h2 = (((x1 * lax.rsqrt(ms + EPS)) * g2_ref[...]) * (1.0 + scale2) + shift2).astype(jnp.bfloat16)
    ff = None
    for j in range(D_FF // FF_CHUNK):
        cols = slice(j * FF_CHUNK, (j + 1) * FF_CHUNK)
        a = jnp.maximum(jnp.dot(h2, w1_ref[:, cols], preferred_element_type=jnp.float32), 0.0)
        part = jnp.dot((a * a).astype(jnp.bfloat16), w2_ref[cols, :], preferred_element_type=jnp.float32)
        ff = part if ff is None else ff + part
    y_ref[...] = x1 + gate2 * ff


def _ffn_call(x, oa, ob, oc, mod, layer, boff, p):
    bsz, seq, _ = x.shape
    tm = TM_FFN
    tok = lambda w: pl.BlockSpec((None, tm, w), lambda b, i: (b, i, 0))
    const2 = lambda a: pl.BlockSpec(a.shape, lambda b, i: (0, 0), pipeline_mode=pl.Buffered(1))
    return pl.pallas_call(
        _ffn_kernel,
        grid=(bsz, seq // tm),
        in_specs=[
            tok(D_MODEL), tok(A_Q), tok(B_W), tok(C_W),
            pl.BlockSpec((None, None, 6, D_MODEL), lambda b, i: (layer, boff + b, 0, 0)),
            const2(p["g2"]), const2(p["w_out"]), const2(p["w_ff1"]), const2(p["w_ff2"]),
        ],
        out_specs=tok(D_MODEL),
        out_shape=jax.ShapeDtypeStruct((bsz, seq, D_MODEL), jnp.float32),
        compiler_params=_cparams(2),
        name="outproj_ffn",
    )(x, oa, ob, oc, mod, p["g2"], p["w_out"], p["w_ff1"], p["w_ff2"])


def _tile_heads(g, n_heads, scale=1.0):
    return (jnp.tile(g.astype(jnp.float32), n_heads) * scale).reshape(1, n_heads * HEAD_DIM)


def _rope_tables(seq):
    half = ROPE_DIMS // 2
    inv = ROPE_THETA ** (-jnp.arange(0, ROPE_DIMS, 2, dtype=jnp.float32) / ROPE_DIMS)
    ang = jnp.arange(seq, dtype=jnp.float32)[:, None] * inv[None, :]
    cos, sin = jnp.cos(ang), jnp.sin(ang)
    one = jnp.ones((seq, HEAD_DIM - ROPE_DIMS), jnp.float32)
    zero_t = jnp.zeros((seq, HEAD_DIM - ROPE_DIMS), jnp.float32)
    zero_h = jnp.zeros((seq, half), jnp.float32)
    rep = LANES_V7X // HEAD_DIM
    cos_t = jnp.tile(jnp.concatenate([cos, cos, one], axis=1), (1, rep))
    sin_up = jnp.tile(jnp.concatenate([-sin, zero_h, zero_t], axis=1), (1, rep))
    sin_dn = jnp.tile(jnp.concatenate([zero_h, sin, zero_t], axis=1), (1, rep))
    return cos_t, sin_up, sin_dn


def _band_mask():
    qi = np.arange(BLOCK)[:, None]
    kj = np.arange(3 * BLOCK)[None, :]
    ok = np.abs(kj - BLOCK - qi) <= BLOCK
    band = np.where(ok, 0.0, NEG_INF).astype(np.float32)
    return jnp.asarray(np.concatenate([band, band], axis=0))


def _nbr_bias_table(rel_bias):
    col = np.arange(GRID_W)
    cs = np.clip(col - NA_KW // 2, 0, GRID_W - NA_KW)
    col_in = (col[None, :] >= cs[:, None]) & (col[None, :] < cs[:, None] + NA_KW)
    dc = np.clip(col[None, :] - col[:, None] + NA_KW - 1, 0, 2 * NA_KW - 2)
    ndc = 2 * NA_KW - 1
    onehot = (dc.reshape(-1)[None, :] == np.arange(ndc)[:, None]).astype(np.float32)
    tm = jnp.einsum("hrd,dx->hrx", rel_bias.astype(jnp.float32), jnp.asarray(onehot),
                    precision=lax.Precision.HIGHEST)
    tm = jnp.where(jnp.asarray(col_in.reshape(-1))[None, None, :], tm, NEG_INF)
    tm = tm.reshape(C_HEADS, 2 * NA_KH - 1, GRID_W, GRID_W)
    pair = jnp.concatenate([tm[:, :-1], tm[:, 1:]], axis=-1)
    pair = pair.reshape(C_HEADS // 2, 2, 2 * NA_KH - 2, GRID_W, 2 * GRID_W)
    return pair.transpose(0, 2, 1, 3, 4).reshape(C_HEADS // 2, 2 * NA_KH - 2, 2 * GRID_W, 2 * GRID_W)


def _layer_params(l, seq_tables, norm1_g, norm2_g, w_in, qa_norm_g, ka_norm_g, a_sink, bv_norm_g,
                  w_spatial, b_spatial, qc_norm_g, kc_norm_g, c_rel_bias, w_out, w_ff1, w_ff2):
    order = np.asarray(A_HEAD_ORDER)
    q_cols = (order[:, None] * HEAD_DIM + np.arange(HEAD_DIM)[None, :]).reshape(-1)
    in_cols = np.concatenate([q_cols, np.arange(A_Q, D_IN)])
    out_rows = np.concatenate([q_cols, np.arange(A_Q, D_MIX)])
    seg = np.kron(np.eye(SEG_W // HEAD_DIM), np.ones((HEAD_DIM, HEAD_DIM))).astype(np.float32)
    cos_t, sin_up, sin_dn = seq_tables
    ws = w_spatial[l].astype(jnp.bfloat16)
    return {
        "g1": norm1_g[l].reshape(1, D_MODEL),
        "g2": norm2_g[l].reshape(1, D_MODEL),
        "w_in": w_in[l][:, in_cols].astype(jnp.bfloat16),
        "w_out": w_out[l][out_rows, :].astype(jnp.bfloat16),
        "w_ff1": w_ff1[l].astype(jnp.bfloat16),
        "w_ff2": w_ff2[l].astype(jnp.bfloat16),
        "seg": jnp.asarray(seg, jnp.bfloat16),
        "ga": jnp.concatenate([_tile_heads(qa_norm_g[l], A_HEADS, ATTN_SCALE),
                               _tile_heads(ka_norm_g[l], A_KV_HEADS)], axis=1),
        "gc": jnp.concatenate([_tile_heads(qc_norm_g[l], C_HEADS, ATTN_SCALE),
                               _tile_heads(kc_norm_g[l], C_HEADS)], axis=1),
        "gbv": bv_norm_g[l].reshape(1, B_W),
        "cos": cos_t, "sin_up": sin_up, "sin_dn": sin_dn,
        "ws": jnp.concatenate([ws[0::2], ws[1::2]], axis=2),
        "bs": jnp.repeat(b_spatial[l].T, HEAD_DIM, axis=1),
        "sink": a_sink[l][order].astype(jnp.float32),
        "nbr_bias": _nbr_bias_table(c_rel_bias[l]),
    }


def _trunk(x, mod, boff, layer_params):
    for l, p in enumerate(layer_params):
        aq, ak, av, ob, cq, ck, cv = _inproj_call(x, mod, l, boff, p)
        oa = _window_call(aq, ak, av, p["sink"], p["band"])
        oc = _nbr_call(cq, ck, cv, p["nbr_bias"])
        x = _ffn_call(x, oa, ob, oc, mod, l, boff, p)
    return x


def kernel(x_prompt, x_sample, c_prompt, c_sample, norm1_g, norm2_g, w_ada, b_ada, w_in, qa_norm_g, ka_norm_g, a_sink, bv_norm_g, w_spatial, b_spatial, qc_norm_g, kc_norm_g, c_rel_bias, w_out, w_ff1, w_ff2):
    depth = w_in.shape[0]
    nb_prompt, seq_prompt, _ = x_prompt.shape
    nb_sample, seq_sample, _ = x_sample.shape
    assert nb_prompt + nb_sample <= 8
    for seq in (seq_prompt, seq_sample):
        assert seq % TM_IN == 0 and seq % TM_FFN == 0 and seq % TQ_WIN == 0 and seq // BLOCK >= 3

    c8 = jnp.zeros((8, D_MODEL), jnp.float32)
    c8 = c8.at[:nb_prompt].set(c_prompt).at[nb_prompt:nb_prompt + nb_sample].set(c_sample)
    mod = _ada_call(c8, w_ada, b_ada).reshape(depth, 8, 6, D_MODEL)

    seq_tables = _rope_tables(max(seq_prompt, seq_sample))
    band = _band_mask()
    layer_params = []
    for l in range(depth):
        p = _layer_params(l, seq_tables, norm1_g, norm2_g, w_in, qa_norm_g, ka_norm_g, a_sink,
                          bv_norm_g, w_spatial, b_spatial, qc_norm_g, kc_norm_g, c_rel_bias,
                          w_out, w_ff1, w_ff2)
        p["band"] = band
        layer_params.append(p)

    y_prompt = _trunk(x_prompt, mod, 0, layer_params)
    y_sample = _trunk(x_sample, mod, nb_prompt, layer_params)
    return (y_prompt, y_sample)
```

```python
from functools import partial

import numpy as np
import jax
import jax.numpy as jnp
from jax import lax
from jax.experimental import pallas as pl
from jax.experimental.pallas import tpu as pltpu

D_MODEL = 1024
HEAD_DIM = 64
A_HEADS = 6
A_KV_HEADS = 2
A_GROUP = A_HEADS // A_KV_HEADS
B_GROUPS = 4
C_HEADS = 6
A_Q = A_HEADS * HEAD_DIM
A_KV = A_KV_HEADS * HEAD_DIM
B_W = B_GROUPS * HEAD_DIM
C_W = C_HEADS * HEAD_DIM
D_IN = A_Q + 2 * A_KV + 2 * B_W + 3 * C_W
D_MIX = A_Q + B_W + C_W
BLOCK = 128
CHUNK = 128
GRID_W = 64
NA_KH = 8
NA_KW = 16
ROPE_THETA = 500000.0
ROPE_DIMS = HEAD_DIM // 4
D_FF = 4 * D_MODEL
EPS = 1e-6
ATTN_SCALE = HEAD_DIM ** -0.5
LOG2E = float(np.log2(np.e))

LANES_V7X = 128
MXU_DIM_V7X = 256
SEG_W = MXU_DIM_V7X
VMEM_LIMIT_BYTES_V7X = 56 * 1024 * 1024

TM_IN = 512
TM_FFN = 512
FF_CHUNK = 1024
TQ_WIN = 1024
TR_NBR = 16
KV_ROWS_BLK = 4
NBR_WIN_BLOCKS = TR_NBR // KV_ROWS_BLK + 2
ADA_TN = 1536
WIN_PIPE_LAGS = (2, 2)
NBR_PIPE_LAGS = (2, 3)

NEG_INF = float("-inf")

A_HEAD_ORDER = (0, 3, 1, 4, 2, 5)


def _cparams(n_axes):
    return pltpu.CompilerParams(
        dimension_semantics=("arbitrary",) * n_axes,
        vmem_limit_bytes=VMEM_LIMIT_BYTES_V7X,
    )


def _lane_is_low(shape):
    lane = lax.broadcasted_iota(jnp.int32, shape, len(shape) - 1)
    return (lane % LANES_V7X) < HEAD_DIM


def _ada_kernel(c_ref, w_ref, b_ref, o_ref):
    c = c_ref[...]
    s = (c * jax.nn.sigmoid(c)).astype(jnp.bfloat16)
    w = w_ref[...].astype(jnp.bfloat16)
    o_ref[...] = jnp.dot(s, w, preferred_element_type=jnp.float32) + b_ref[...]


def _ada_call(c8, w_ada, b_ada):
    depth = w_ada.shape[0]
    n = w_ada.shape[2]
    return pl.pallas_call(
        _ada_kernel,
        grid=(depth, n // ADA_TN),
        in_specs=[
            pl.BlockSpec((8, D_MODEL), lambda l, j: (0, 0)),
            pl.BlockSpec((None, D_MODEL, ADA_TN), lambda l, j: (l, 0, j)),
            pl.BlockSpec((None, 1, ADA_TN), lambda l, j: (l, 0, j)),
        ],
        out_specs=pl.BlockSpec((None, 8, ADA_TN), lambda l, j: (l, 0, j)),
        out_shape=jax.ShapeDtypeStruct((depth, 8, n), jnp.float32),
        compiler_params=_cparams(2),
        name="ada_mod",
    )(c8, w_ada, b_ada.reshape(depth, 1, n))


def _head_rms(x, seg_ones, gain):
    outs = []
    for j in range(x.shape[1] // SEG_W):
        xj = x[:, j * SEG_W:(j + 1) * SEG_W]
        ssum = jnp.dot((xj * xj).astype(jnp.bfloat16), seg_ones, preferred_element_type=jnp.float32)
        outs.append((xj * lax.rsqrt(ssum * (1.0 / HEAD_DIM) + EPS)) * gain[:, j * SEG_W:(j + 1) * SEG_W])
    return jnp.concatenate(outs, axis=1)


def _rope(x, cos_t, sin_up, sin_dn):
    n = x.shape[1] // LANES_V7X
    outs = []
    for j in range(n):
        xj = x[:, j * LANES_V7X:(j + 1) * LANES_V7X]
        up = pltpu.roll(xj, LANES_V7X - ROPE_DIMS // 2, 1)
        dn = pltpu.roll(xj, ROPE_DIMS // 2, 1)
        outs.append(xj * cos_t + up * sin_up + dn * sin_dn)
    return outs[0] if n == 1 else jnp.concatenate(outs, axis=1)


def _inproj_kernel(x_ref, mod_ref, g1_ref, w_ref, seg_ref, ga_ref, gc_ref,
                   gbv_ref, cos_ref, sup_ref, sdn_ref, ws_ref, bs_ref,
                   aq_ref, ak_ref, av_ref, ob_ref, cq_ref, ck_ref, cv_ref):
    x = x_ref[...]
    shift1 = mod_ref[0:1, :]
    scale1 = mod_ref[1:2, :]
    ms = jnp.mean(x * x, axis=-1, keepdims=True)
    h = ((x * lax.rsqrt(ms + EPS)) * g1_ref[...]) * (1.0 + scale1) + shift1
    proj = jnp.dot(h.astype(jnp.bfloat16), w_ref[...], preferred_element_type=jnp.float32)

    seg = seg_ref[...]
    o_av = A_Q + A_KV
    o_bu = o_av + A_KV
    o_bv = o_bu + B_W
    o_cq = o_bv + B_W
    o_cv = o_cq + 2 * C_W

    a_n = _rope(_head_rms(proj[:, 0:o_av], seg, ga_ref[...]), cos_ref[...], sup_ref[...], sdn_ref[...])
    aq_ref[...] = a_n[:, :A_Q].astype(jnp.bfloat16)
    ak_ref[...] = a_n[:, A_Q:].astype(jnp.bfloat16)
    av_ref[...] = proj[:, o_av:o_bu].astype(jnp.bfloat16)

    c_n = _head_rms(proj[:, o_cq:o_cv], seg, gc_ref[...])
    cq_ref[...] = c_n[:, :C_W].astype(jnp.bfloat16)
    ck_ref[...] = c_n[:, C_W:].astype(jnp.bfloat16)
    cv_ref[...] = proj[:, o_cv:].astype(jnp.bfloat16)

    u = jax.nn.gelu(proj[:, o_bu:o_bv])
    v = jax.nn.gelu(proj[:, o_bv:o_cq])
    vms = jnp.mean(v * v, axis=-1, keepdims=True)
    vr = ((v * lax.rsqrt(vms + EPS)) * gbv_ref[...]).astype(jnp.bfloat16)
    low = _lane_is_low((CHUNK, LANES_V7X))
    zero = jnp.zeros((), jnp.bfloat16)
    tm = x.shape[0]
    for c in range(tm // CHUNK):
        rows = slice(c * CHUNK, (c + 1) * CHUNK)
        sv_parts = []
        for j in range(B_W // LANES_V7X):
            vj = vr[rows, j * LANES_V7X:(j + 1) * LANES_V7X]
            v2 = jnp.concatenate([jnp.where(low, vj, zero), jnp.where(low, zero, vj)], axis=0)
            sv_parts.append(jnp.dot(ws_ref[j], v2, preferred_element_type=jnp.float32))
        sv = jnp.concatenate(sv_parts, axis=1) + bs_ref[...]
        ob_ref[rows, :] = (u[rows, :] * sv).astype(jnp.bfloat16)


def _inproj_call(x, mod, layer, boff, p):
    bsz, seq, _ = x.shape
    tm = TM_IN
    tok = lambda w: pl.BlockSpec((None, tm, w), lambda b, i: (b, i, 0))
    const2 = lambda a: pl.BlockSpec(a.shape, lambda b, i: (0, 0))
    bf = lambda w: jax.ShapeDtypeStruct((bsz, seq, w), jnp.bfloat16)
    rope_spec = pl.BlockSpec((tm, LANES_V7X), lambda b, i: (i, 0))
    return pl.pallas_call(
        _inproj_kernel,
        grid=(bsz, seq // tm),
        in_specs=[
            tok(D_MODEL),
            pl.BlockSpec((None, None, 6, D_MODEL), lambda b, i: (layer, boff + b, 0, 0)),
            const2(p["g1"]), const2(p["w_in"]), const2(p["seg"]),
            const2(p["ga"]), const2(p["gc"]), const2(p["gbv"]),
            rope_spec, rope_spec, rope_spec,
            pl.BlockSpec(p["ws"].shape, lambda b, i: (0, 0, 0)),
            const2(p["bs"]),
        ],
        out_specs=[tok(A_Q), tok(A_KV), tok(A_KV), tok(B_W), tok(C_W), tok(C_W), tok(C_W)],
        out_shape=[bf(A_Q), bf(A_KV), bf(A_KV), bf(B_W), bf(C_W), bf(C_W), bf(C_W)],
        compiler_params=_cparams(2),
        name="inproj",
    )(x, mod, p["g1"], p["w_in"], p["seg"], p["ga"], p["gc"], p["gbv"],
      p["cos"], p["sin_up"], p["sin_dn"], p["ws"], p["bs"])


_NT_DIMS = (((1,), (1,)), ((), ()))


def _qk_two_heads(q, k, low):
    zero = jnp.zeros((), q.dtype)
    q2 = jnp.concatenate([jnp.where(low, q, zero), jnp.where(low, zero, q)], axis=0)
    return lax.dot_general(q2, k, _NT_DIMS, preferred_element_type=jnp.float32)


def _softmax_numerator(s, extra_logit=None):
    m = jnp.max(s, axis=-1, keepdims=True)
    if extra_logit is not None:
        m = jnp.maximum(m, extra_logit)
    e = jnp.exp2(s - m)
    l = jnp.sum(e, axis=-1, keepdims=True)
    if extra_logit is not None:
        l = l + jnp.exp2(extra_logit - m)
    return e.astype(jnp.bfloat16), l


def _pv_two_heads(e2, l2, v, low):
    o2 = jnp.dot(e2, v, preferred_element_type=jnp.float32) / l2
    n = o2.shape[0] // 2
    return jnp.where(low, o2[:n], o2[n:]).astype(jnp.bfloat16)


def _software_pipeline(n, stages, lags):
    offs = [0]
    for lag in lags:
        offs.append(offs[-1] + lag)
    vals = [dict() for _ in stages]
    for step in range(n + offs[-1]):
        for si, fn in enumerate(stages):
            u = step - offs[si]
            if 0 <= u < n:
                vals[si][u] = fn(u) if si == 0 else fn(u, vals[si - 1].pop(u))


def _window_kernel(sink_ref, band_ref, q_ref, kp_ref, kc_ref, kn_ref, vp_ref, vc_ref, vn_ref, o_ref,
                   kwin, vwin):
    i = pl.program_id(1)
    last = pl.num_programs(1) - 1
    kwin[0:BLOCK, :] = kp_ref[...]
    kwin[BLOCK:BLOCK + TQ_WIN, :] = kc_ref[...]
    kwin[BLOCK + TQ_WIN:, :] = kn_ref[...]
    vwin[0:BLOCK, :] = vp_ref[...]
    vwin[BLOCK:BLOCK + TQ_WIN, :] = vc_ref[...]
    vwin[BLOCK + TQ_WIN:, :] = vn_ref[...]
    band = band_ref[...]
    col = lax.broadcasted_iota(jnp.int32, band.shape, 1)
    is_low_head = lax.broadcasted_iota(jnp.int32, (2 * BLOCK, 1), 0) < BLOCK
    low = _lane_is_low((BLOCK, LANES_V7X))
    nt = TQ_WIN // BLOCK
    ng = A_Q // LANES_V7X
    mask_first = band + jnp.where(col < BLOCK, jnp.where(i == 0, NEG_INF, 0.0), 0.0)
    mask_last = band + jnp.where(col >= 2 * BLOCK, jnp.where(i == last, NEG_INF, 0.0), 0.0)

    def unit(u):
        t, g = divmod(u, ng)
        return t, g, slice(t * BLOCK, (t + 1) * BLOCK), slice(g * LANES_V7X, (g + 1) * LANES_V7X)

    def qk(u):
        t, g, rows, cols = unit(u)
        return _qk_two_heads(q_ref[rows, cols], kwin[t * BLOCK:(t + 3) * BLOCK, :], low)

    def sm(u, s):
        t, g, rows, cols = unit(u)
        mask = mask_first if t == 0 else (mask_last if t == nt - 1 else band)
        sink = jnp.where(is_low_head, sink_ref[2 * g], sink_ref[2 * g + 1])
        return _softmax_numerator(s + mask, sink)

    def pv(u, e):
        t, g, rows, cols = unit(u)
        o_ref[rows, cols] = _pv_two_heads(*e, vwin[t * BLOCK:(t + 3) * BLOCK, :], low)

    _software_pipeline(nt * ng, (qk, sm, pv), WIN_PIPE_LAGS)


def _window_call(aq, ak, av, sink, band):
    bsz, seq, _ = aq.shape
    nt = TQ_WIN // BLOCK
    nb = seq // BLOCK
    assert nt >= 2
    cur = lambda w: pl.BlockSpec((None, TQ_WIN, w), lambda b, i: (b, i, 0))
    prev = pl.BlockSpec((None, BLOCK, A_KV), lambda b, i: (b, jnp.maximum(i * nt - 1, 0), 0))
    nxt = pl.BlockSpec((None, BLOCK, A_KV), lambda b, i: (b, jnp.minimum(i * nt + nt, nb - 1), 0))
    win = pltpu.VMEM((TQ_WIN + 2 * BLOCK, A_KV), jnp.bfloat16)
    return pl.pallas_call(
        _window_kernel,
        grid=(bsz, seq // TQ_WIN),
        in_specs=[
            pl.BlockSpec(memory_space=pltpu.SMEM),
            pl.BlockSpec(band.shape, lambda b, i: (0, 0)),
            cur(A_Q), prev, cur(A_KV), nxt, prev, cur(A_KV), nxt,
        ],
        out_specs=cur(A_Q),
        out_shape=jax.ShapeDtypeStruct((bsz, seq, A_Q), jnp.bfloat16),
        scratch_shapes=[win, win],
        compiler_params=_cparams(2),
        name="window_attn",
    )(sink, band, aq, ak, ak, ak, av, av, av)


def _nbr_kernel(rows, bias_ref, q_ref, *refs):
    nwin = NBR_WIN_BLOCKS
    k_blocks, v_blocks = refs[:nwin], refs[nwin:2 * nwin]
    o_ref, kwin, vwin = refs[2 * nwin:]
    i = pl.program_id(1)
    blk = KV_ROWS_BLK * GRID_W
    for n in range(nwin):
        kwin[n * blk:(n + 1) * blk, :] = k_blocks[n][...]
        vwin[n * blk:(n + 1) * blk, :] = v_blocks[n][...]
    r0 = i * TR_NBR
    win_start = _nbr_window_block(i, rows) * KV_ROWS_BLK
    low = _lane_is_low((GRID_W, LANES_V7X))
    nkeys = NA_KH * GRID_W
    ng = C_W // LANES_V7X
    key_off, bias_base = [], []
    for t in range(TR_NBR):
        r = r0 + t
        rs = jnp.clip(r - NA_KH // 2, 0, rows - NA_KH)
        key_off.append(pl.multiple_of((rs - win_start) * GRID_W, GRID_W))
        bias_base.append(NA_KH - 1 - (r - rs))

    def unit(u):
        t, g = divmod(u, ng)
        return t, g, slice(t * GRID_W, (t + 1) * GRID_W), slice(g * LANES_V7X, (g + 1) * LANES_V7X)

    def qk(u):
        t, g, qrows, cols = unit(u)
        return _qk_two_heads(q_ref[qrows, cols], kwin[pl.ds(key_off[t], nkeys), cols], low)

    def sm(u, s):
        t, g, qrows, cols = unit(u)
        bias = jnp.concatenate([bias_ref[g, bias_base[t] + 2 * m] for m in range(NA_KH // 2)], axis=1)
        return _softmax_numerator(s + bias)

    def pv(u, e):
        t, g, qrows, cols = unit(u)
        o_ref[qrows, cols] = _pv_two_heads(*e, vwin[pl.ds(key_off[t], nkeys), cols], low)

    _software_pipeline(TR_NBR * ng, (qk, sm, pv), NBR_PIPE_LAGS)


def _nbr_window_block(i, rows):
    per_tile = TR_NBR // KV_ROWS_BLK
    nblk = rows // KV_ROWS_BLK
    return jnp.clip(i * per_tile - 1, 0, nblk - (per_tile + 2))


def _nbr_call(cq, ck, cv, bias_tab):
    bsz, seq, _ = cq.shape
    rows = seq // GRID_W
    tq = TR_NBR * GRID_W
    blk = KV_ROWS_BLK * GRID_W
    nwin = NBR_WIN_BLOCKS
    assert rows % TR_NBR == 0 and rows >= nwin * KV_ROWS_BLK and NA_KH // 2 <= KV_ROWS_BLK
    kv_spec = lambda n: pl.BlockSpec((None, blk, C_W), lambda b, i: (b, _nbr_window_block(i, rows) + n, 0))
    q_spec = pl.BlockSpec((None, tq, C_W), lambda b, i: (b, i, 0))
    kv_specs = [kv_spec(n) for n in range(nwin)]
    return pl.pallas_call(
        partial(_nbr_kernel, rows),
        grid=(bsz, rows // TR_NBR),
        in_specs=[pl.BlockSpec(bias_tab.shape, lambda b, i: (0, 0, 0, 0)), q_spec] + kv_specs + kv_specs,
        out_specs=q_spec,
        out_shape=jax.ShapeDtypeStruct((bsz, seq, C_W), jnp.bfloat16),
        scratch_shapes=[pltpu.VMEM((nwin * blk, C_W), jnp.bfloat16),
                        pltpu.VMEM((nwin * blk, C_W), jnp.bfloat16)],
        compiler_params=_cparams(2),
        name="nbr_attn",
    )(bias_tab, cq, *([ck] * nwin), *([cv] * nwin))


def _ffn_kernel(x_ref, oa_ref, ob_ref, oc_ref, mod_ref, g2_ref, wo_ref, w1_ref, w2_ref, y_ref):
    gate1 = mod_ref[2:3, :]
    shift2 = mod_ref[3:4, :]
    scale2 = mod_ref[4:5, :]
    gate2 = mod_ref[5:6, :]
    mix_in = jnp.concatenate([oa_ref[...], ob_ref[...], oc_ref[...]], axis=1)
    mix = jnp.dot(mix_in, wo_ref[...], preferred_element_type=jnp.float32)
    x1 = x_ref[...] + gate1 * mix
    ms = jnp.mean(x1 * x1, axis=-1, keepdims=True)
    h2 = (((x1 * lax.rsqrt(ms + EPS)) * g2_ref[...]) * (1.0 + scale2) + shift2).astype(jnp.bfloat16)
    ff = None
    for j in range(D_FF // FF_CHUNK):
        cols = slice(j * FF_CHUNK, (j + 1) * FF_CHUNK)
        a = jnp.maximum(jnp.dot(h2, w1_ref[:, cols], preferred_element_type=jnp.float32), 0.0)
        part = jnp.dot((a * a).astype(jnp.bfloat16), w2_ref[cols, :], preferred_element_type=jnp.float32)
        ff = part if ff is None else ff + part
    y_ref[...] = x1 + gate2 * ff


def _ffn_call(x, oa, ob, oc, mod, layer, boff, p):
    bsz, seq, _ = x.shape
    tm = TM_FFN
    tok = lambda w: pl.BlockSpec((None, tm, w), lambda b, i: (b, i, 0))
    const2 = lambda a: pl.BlockSpec(a.shape, lambda b, i: (0, 0), pipeline_mode=pl.Buffered(1))
    return pl.pallas_call(
        _ffn_kernel,
        grid=(bsz, seq // tm),
        in_specs=[
            tok(D_MODEL), tok(A_Q), tok(B_W), tok(C_W),
            pl.BlockSpec((None, None, 6, D_MODEL), lambda b, i: (layer, boff + b, 0, 0)),
            const2(p["g2"]), const2(p["w_out"]), const2(p["w_ff1"]), const2(p["w_ff2"]),
        ],
        out_specs=tok(D_MODEL),
        out_shape=jax.ShapeDtypeStruct((bsz, seq, D_MODEL), jnp.float32),
        compiler_params=_cparams(2),
        name="outproj_ffn",
    )(x, oa, ob, oc, mod, p["g2"], p["w_out"], p["w_ff1"], p["w_ff2"])


def _tile_heads(g, n_heads, scale=1.0):
    return (jnp.tile(g.astype(jnp.float32), n_heads) * scale).reshape(1, n_heads * HEAD_DIM)


def _rope_tables(seq):
    half = ROPE_DIMS // 2
    inv = ROPE_THETA ** (-jnp.arange(0, ROPE_DIMS, 2, dtype=jnp.float32) / ROPE_DIMS)
    ang = jnp.arange(seq, dtype=jnp.float32)[:, None] * inv[None, :]
    cos, sin = jnp.cos(ang), jnp.sin(ang)
    one = jnp.ones((seq, HEAD_DIM - ROPE_DIMS), jnp.float32)
    zero_t = jnp.zeros((seq, HEAD_DIM - ROPE_DIMS), jnp.float32)
    zero_h = jnp.zeros((seq, half), jnp.float32)
    rep = LANES_V7X // HEAD_DIM
    cos_t = jnp.tile(jnp.concatenate([cos, cos, one], axis=1), (1, rep))
    sin_up = jnp.tile(jnp.concatenate([-sin, zero_h, zero_t], axis=1), (1, rep))
    sin_dn = jnp.tile(jnp.concatenate([zero_h, sin, zero_t], axis=1), (1, rep))
    return cos_t, sin_up, sin_dn


def _band_mask():
    qi = np.arange(BLOCK)[:, None]
    kj = np.arange(3 * BLOCK)[None, :]
    ok = np.abs(kj - BLOCK - qi) <= BLOCK
    band = np.where(ok, 0.0, NEG_INF).astype(np.float32)
    return jnp.asarray(np.concatenate([band, band], axis=0))


def _nbr_bias_table(rel_bias):
    col = np.arange(GRID_W)
    cs = np.clip(col - NA_KW // 2, 0, GRID_W - NA_KW)
    col_in = (col[None, :] >= cs[:, None]) & (col[None, :] < cs[:, None] + NA_KW)
    dc = np.clip(col[None, :] - col[:, None] + NA_KW - 1, 0, 2 * NA_KW - 2)
    ndc = 2 * NA_KW - 1
    onehot = (dc.reshape(-1)[None, :] == np.arange(ndc)[:, None]).astype(np.float32)
    tm = jnp.einsum("hrd,dx->hrx", rel_bias.astype(jnp.float32), jnp.asarray(onehot),
                    precision=lax.Precision.HIGHEST)
    tm = jnp.where(jnp.asarray(col_in.reshape(-1))[None, None, :], tm * LOG2E, NEG_INF)
    tm = tm.reshape(C_HEADS, 2 * NA_KH - 1, GRID_W, GRID_W)
    pair = jnp.concatenate([tm[:, :-1], tm[:, 1:]], axis=-1)
    pair = pair.reshape(C_HEADS // 2, 2, 2 * NA_KH - 2, GRID_W, 2 * GRID_W)
    return pair.transpose(0, 2, 1, 3, 4).reshape(C_HEADS // 2, 2 * NA_KH - 2, 2 * GRID_W, 2 * GRID_W)


def _layer_params(l, seq_tables, norm1_g, norm2_g, w_in, qa_norm_g, ka_norm_g, a_sink, bv_norm_g,
                  w_spatial, b_spatial, qc_norm_g, kc_norm_g, c_rel_bias, w_out, w_ff1, w_ff2):
    order = np.asarray(A_HEAD_ORDER)
    q_cols = (order[:, None] * HEAD_DIM + np.arange(HEAD_DIM)[None, :]).reshape(-1)
    in_cols = np.concatenate([q_cols, np.arange(A_Q, D_IN)])
    out_rows = np.concatenate([q_cols, np.arange(A_Q, D_MIX)])
    seg = np.kron(np.eye(SEG_W // HEAD_DIM), np.ones((HEAD_DIM, HEAD_DIM))).astype(np.float32)
    cos_t, sin_up, sin_dn = seq_tables
    ws = w_spatial[l].astype(jnp.bfloat16)
    return {
        "g1": norm1_g[l].reshape(1, D_MODEL),
        "g2": norm2_g[l].reshape(1, D_MODEL),
        "w_in": w_in[l][:, in_cols].astype(jnp.bfloat16),
        "w_out": w_out[l][out_rows, :].astype(jnp.bfloat16),
        "w_ff1": w_ff1[l].astype(jnp.bfloat16),
        "w_ff2": w_ff2[l].astype(jnp.bfloat16),
        "seg": jnp.asarray(seg, jnp.bfloat16),
        "ga": jnp.concatenate([_tile_heads(qa_norm_g[l], A_HEADS, ATTN_SCALE * LOG2E),
                               _tile_heads(ka_norm_g[l], A_KV_HEADS)], axis=1),
        "gc": jnp.concatenate([_tile_heads(qc_norm_g[l], C_HEADS, ATTN_SCALE * LOG2E),
                               _tile_heads(kc_norm_g[l], C_HEADS)], axis=1),
        "gbv": bv_norm_g[l].reshape(1, B_W),
        "cos": cos_t, "sin_up": sin_up, "sin_dn": sin_dn,
        "ws": jnp.concatenate([ws[0::2], ws[1::2]], axis=2),
        "bs": jnp.repeat(b_spatial[l].T, HEAD_DIM, axis=1),
        "sink": a_sink[l][order].astype(jnp.float32) * LOG2E,
        "nbr_bias": _nbr_bias_table(c_rel_bias[l]),
    }


def _trunk(x, mod, boff, layer_params):
    for l, p in enumerate(layer_params):
        aq, ak, av, ob, cq, ck, cv = _inproj_call(x, mod, l, boff, p)
        oa = _window_call(aq, ak, av, p["sink"], p["band"])
        oc = _nbr_call(cq, ck, cv, p["nbr_bias"])
        x = _ffn_call(x, oa, ob, oc, mod, l, boff, p)
    return x


def kernel(x_prompt, x_sample, c_prompt, c_sample, norm1_g, norm2_g, w_ada, b_ada, w_in, qa_norm_g, ka_norm_g, a_sink, bv_norm_g, w_spatial, b_spatial, qc_norm_g, kc_norm_g, c_rel_bias, w_out, w_ff1, w_ff2):
    depth = w_in.shape[0]
    nb_prompt, seq_prompt, _ = x_prompt.shape
    nb_sample, seq_sample, _ = x_sample.shape
    assert nb_prompt + nb_sample <= 8
    for seq in (seq_prompt, seq_sample):
        assert seq % TM_IN == 0 and seq % TM_FFN == 0 and seq % TQ_WIN == 0 and seq // BLOCK >= 3

    c8 = jnp.zeros((8, D_MODEL), jnp.float32)
    c8 = c8.at[:nb_prompt].set(c_prompt).at[nb_prompt:nb_prompt + nb_sample].set(c_sample)
    mod = _ada_call(c8, w_ada, b_ada).reshape(depth, 8, 6, D_MODEL)

    seq_tables = _rope_tables(max(seq_prompt, seq_sample))
    band = _band_mask()
    layer_params = []
    for l in range(depth):
        p = _layer_params(l, seq_tables, norm1_g, norm2_g, w_in, qa_norm_g, ka_norm_g, a_sink,
                          bv_norm_g, w_spatial, b_spatial, qc_norm_g, kc_norm_g, c_rel_bias,
                          w_out, w_ff1, w_ff2)
        p["band"] = band
        layer_params.append(p)

    y_prompt = _trunk(x_prompt, mod, 0, layer_params)
    y_sample = _trunk(x_sample, mod, nb_prompt, layer_params)
    return (y_prompt, y_sample)
```

```python
from functools import partial

import numpy as np
import jax
import jax.numpy as jnp
from jax import lax
from jax.experimental import pallas as pl
from jax.experimental.pallas import tpu as pltpu

D_MODEL = 1024
HEAD_DIM = 64
A_HEADS = 6
A_KV_HEADS = 2
A_GROUP = A_HEADS // A_KV_HEADS
B_GROUPS = 4
C_HEADS = 6
A_Q = A_HEADS * HEAD_DIM
A_KV = A_KV_HEADS * HEAD_DIM
B_W = B_GROUPS * HEAD_DIM
C_W = C_HEADS * HEAD_DIM
D_IN = A_Q + 2 * A_KV + 2 * B_W + 3 * C_W
D_MIX = A_Q + B_W + C_W
BLOCK = 128
CHUNK = 128
GRID_W = 64
NA_KH = 8
NA_KW = 16
ROPE_THETA = 500000.0
ROPE_DIMS = HEAD_DIM // 4
D_FF = 4 * D_MODEL
EPS = 1e-6
ATTN_SCALE = HEAD_DIM ** -0.5
LOG2E = float(np.log2(np.e))

LANES_V7X = 128
MXU_DIM_V7X = 256
SEG_W = MXU_DIM_V7X
VMEM_LIMIT_BYTES_V7X = 56 * 1024 * 1024

TM_IN = 1024
TM_FFN = 1024
FF_CHUNK = 1024
TQ_WIN = 1024
TR_NBR = 16
KV_ROWS_BLK = 4
NBR_WIN_BLOCKS = TR_NBR // KV_ROWS_BLK + 2
ADA_TN = 1536
WIN_PIPE_LAGS = (2, 2)
NBR_PIPE_LAGS = (2, 3)

NEG_INF = float("-inf")

A_HEAD_ORDER = (0, 3, 1, 4, 2, 5)


def _cparams(n_axes):
    return pltpu.CompilerParams(
        dimension_semantics=("arbitrary",) * n_axes,
        vmem_limit_bytes=VMEM_LIMIT_BYTES_V7X,
    )


def _lane_is_low(shape):
    lane = lax.broadcasted_iota(jnp.int32, shape, len(shape) - 1)
    return (lane % LANES_V7X) < HEAD_DIM


def _ada_kernel(c_ref, w_ref, b_ref, o_ref):
    c = c_ref[...]
    s = (c * jax.nn.sigmoid(c)).astype(jnp.bfloat16)
    w = w_ref[...].astype(jnp.bfloat16)
    o_ref[...] = jnp.dot(s, w, preferred_element_type=jnp.float32) + b_ref[...]


def _ada_call(c8, w_ada, b_ada):
    depth = w_ada.shape[0]
    n = w_ada.shape[2]
    return pl.pallas_call(
        _ada_kernel,
        grid=(depth, n // ADA_TN),
        in_specs=[
            pl.BlockSpec((8, D_MODEL), lambda l, j: (0, 0)),
            pl.BlockSpec((None, D_MODEL, ADA_TN), lambda l, j: (l, 0, j)),
            pl.BlockSpec((None, 1, ADA_TN), lambda l, j: (l, 0, j)),
        ],
        out_specs=pl.BlockSpec((None, 8, ADA_TN), lambda l, j: (l, 0, j)),
        out_shape=jax.ShapeDtypeStruct((depth, 8, n), jnp.float32),
        compiler_params=_cparams(2),
        name="ada_mod",
    )(c8, w_ada, b_ada.reshape(depth, 1, n))


def _head_rms(x, seg_ones, gain):
    outs = []
    for j in range(x.shape[1] // SEG_W):
        xj = x[:, j * SEG_W:(j + 1) * SEG_W]
        ssum = jnp.dot((xj * xj).astype(jnp.bfloat16), seg_ones, preferred_element_type=jnp.float32)
        outs.append((xj * lax.rsqrt(ssum * (1.0 / HEAD_DIM) + EPS)) * gain[:, j * SEG_W:(j + 1) * SEG_W])
    return jnp.concatenate(outs, axis=1)


def _rope(x, cos_t, sin_up, sin_dn):
    n = x.shape[1] // LANES_V7X
    outs = []
    for j in range(n):
        xj = x[:, j * LANES_V7X:(j + 1) * LANES_V7X]
        up = pltpu.roll(xj, LANES_V7X - ROPE_DIMS // 2, 1)
        dn = pltpu.roll(xj, ROPE_DIMS // 2, 1)
        outs.append(xj * cos_t + up * sin_up + dn * sin_dn)
    return outs[0] if n == 1 else jnp.concatenate(outs, axis=1)


def _inproj_kernel(x_ref, mod_ref, g1_ref, w_ref, seg_ref, ga_ref, gc_ref,
                   gbv_ref, cos_ref, sup_ref, sdn_ref, ws_ref, bs_ref,
                   aq_ref, ak_ref, av_ref, ob_ref, cq_ref, ck_ref, cv_ref):
    x = x_ref[...]
    shift1 = mod_ref[0:1, :]
    scale1 = mod_ref[1:2, :]
    ms = jnp.mean(x * x, axis=-1, keepdims=True)
    h = ((x * lax.rsqrt(ms + EPS)) * g1_ref[...]) * (1.0 + scale1) + shift1
    proj = jnp.dot(h.astype(jnp.bfloat16), w_ref[...], preferred_element_type=jnp.float32)

    seg = seg_ref[...]
    o_av = A_Q + A_KV
    o_bu = o_av + A_KV
    o_bv = o_bu + B_W
    o_cq = o_bv + B_W
    o_cv = o_cq + 2 * C_W

    a_n = _rope(_head_rms(proj[:, 0:o_av], seg, ga_ref[...]), cos_ref[...], sup_ref[...], sdn_ref[...])
    aq_ref[...] = a_n[:, :A_Q].astype(jnp.bfloat16)
    ak_ref[...] = a_n[:, A_Q:].astype(jnp.bfloat16)
    av_ref[...] = proj[:, o_av:o_bu].astype(jnp.bfloat16)

    c_n = _head_rms(proj[:, o_cq:o_cv], seg, gc_ref[...])
    cq_ref[...] = c_n[:, :C_W].astype(jnp.bfloat16)
    ck_ref[...] = c_n[:, C_W:].astype(jnp.bfloat16)
    cv_ref[...] = proj[:, o_cv:].astype(jnp.bfloat16)

    u = jax.nn.gelu(proj[:, o_bu:o_bv])
    v = jax.nn.gelu(proj[:, o_bv:o_cq])
    vms = jnp.mean(v * v, axis=-1, keepdims=True)
    vr = ((v * lax.rsqrt(vms + EPS)) * gbv_ref[...]).astype(jnp.bfloat16)
    low = _lane_is_low((CHUNK, LANES_V7X))
    zero = jnp.zeros((), jnp.bfloat16)
    tm = x.shape[0]
    for c in range(tm // CHUNK):
        rows = slice(c * CHUNK, (c + 1) * CHUNK)
        sv_parts = []
        for j in range(B_W // LANES_V7X):
            vj = vr[rows, j * LANES_V7X:(j + 1) * LANES_V7X]
            v2 = jnp.concatenate([jnp.where(low, vj, zero), jnp.where(low, zero, vj)], axis=0)
            sv_parts.append(jnp.dot(ws_ref[j], v2, preferred_element_type=jnp.float32))
        sv = jnp.concatenate(sv_parts, axis=1) + bs_ref[...]
        ob_ref[rows, :] = (u[rows, :] * sv).astype(jnp.bfloat16)


def _inproj_call(x, mod, layer, boff, p):
    bsz, seq, _ = x.shape
    tm = TM_IN
    tok = lambda w: pl.BlockSpec((None, tm, w), lambda b, i: (b, i, 0))
    const2 = lambda a: pl.BlockSpec(a.shape, lambda b, i: (0, 0))
    bf = lambda w: jax.ShapeDtypeStruct((bsz, seq, w), jnp.bfloat16)
    rope_spec = pl.BlockSpec((tm, LANES_V7X), lambda b, i: (i, 0))
    return pl.pallas_call(
        _inproj_kernel,
        grid=(bsz, seq // tm),
        in_specs=[
            tok(D_MODEL),
            pl.BlockSpec((None, None, 6, D_MODEL), lambda b, i: (layer, boff + b, 0, 0)),
            const2(p["g1"]), const2(p["w_in"]), const2(p["seg"]),
            const2(p["ga"]), const2(p["gc"]), const2(p["gbv"]),
            rope_spec, rope_spec, rope_spec,
            pl.BlockSpec(p["ws"].shape, lambda b, i: (0, 0, 0)),
            const2(p["bs"]),
        ],
        out_specs=[tok(A_Q), tok(A_KV), tok(A_KV), tok(B_W), tok(C_W), tok(C_W), tok(C_W)],
        out_shape=[bf(A_Q), bf(A_KV), bf(A_KV), bf(B_W), bf(C_W), bf(C_W), bf(C_W)],
        compiler_params=_cparams(2),
        name="inproj",
    )(x, mod, p["g1"], p["w_in"], p["seg"], p["ga"], p["gc"], p["gbv"],
      p["cos"], p["sin_up"], p["sin_dn"], p["ws"], p["bs"])


_NT_DIMS = (((1,), (1,)), ((), ()))


def _qk_two_heads(q, k, low):
    zero = jnp.zeros((), q.dtype)
    q2 = jnp.concatenate([jnp.where(low, q, zero), jnp.where(low, zero, q)], axis=0)
    return lax.dot_general(q2, k, _NT_DIMS, preferred_element_type=jnp.float32)


def _softmax_numerator(s, extra_logit=None):
    m = jnp.max(s, axis=-1, keepdims=True)
    if extra_logit is None:
        return jnp.exp2(s - m).astype(jnp.bfloat16), None
    m = jnp.maximum(m, extra_logit)
    return jnp.exp2(s - m).astype(jnp.bfloat16), jnp.exp2(extra_logit - m)


def _pv_two_heads(e2, extra, v_ones, low):
    r = jnp.dot(e2, v_ones, preferred_element_type=jnp.float32)
    l2 = r[:, LANES_V7X:]
    if extra is not None:
        l2 = l2 + extra
    o2 = r[:, :LANES_V7X] / l2
    n = o2.shape[0] // 2
    return jnp.where(low, o2[:n], o2[n:]).astype(jnp.bfloat16)


def _software_pipeline(n, stages, lags):
    offs = [0]
    for lag in lags:
        offs.append(offs[-1] + lag)
    vals = [dict() for _ in stages]
    for step in range(n + offs[-1]):
        for si, fn in enumerate(stages):
            u = step - offs[si]
            if 0 <= u < n:
                vals[si][u] = fn(u) if si == 0 else fn(u, vals[si - 1].pop(u))


def _window_kernel(sink_ref, band_ref, q_ref, kp_ref, kc_ref, kn_ref, vp_ref, vc_ref, vn_ref, o_ref,
                   kwin, vwin):
    i = pl.program_id(1)
    last = pl.num_programs(1) - 1
    kwin[0:BLOCK, :] = kp_ref[...]
    kwin[BLOCK:BLOCK + TQ_WIN, :] = kc_ref[...]
    kwin[BLOCK + TQ_WIN:, :] = kn_ref[...]
    vwin[0:BLOCK, :A_KV] = vp_ref[...]
    vwin[BLOCK:BLOCK + TQ_WIN, :A_KV] = vc_ref[...]
    vwin[BLOCK + TQ_WIN:, :A_KV] = vn_ref[...]
    vwin[:, A_KV:] = jnp.ones((TQ_WIN + 2 * BLOCK, LANES_V7X), jnp.bfloat16)
    band_prev = band_ref[:, :BLOCK]
    band_next = band_ref[:, 2 * BLOCK:]
    is_low_head = lax.broadcasted_iota(jnp.int32, (2 * BLOCK, 1), 0) < BLOCK
    low = _lane_is_low((BLOCK, LANES_V7X))
    nt = TQ_WIN // BLOCK
    ng = A_Q // LANES_V7X
    band_prev_first = band_prev + jnp.where(i == 0, NEG_INF, 0.0)
    band_next_last = band_next + jnp.where(i == last, NEG_INF, 0.0)

    def unit(u):
        t, g = divmod(u, ng)
        return t, g, slice(t * BLOCK, (t + 1) * BLOCK), slice(g * LANES_V7X, (g + 1) * LANES_V7X)

    def qk(u):
        t, g, rows, cols = unit(u)
        return _qk_two_heads(q_ref[rows, cols], kwin[t * BLOCK:(t + 3) * BLOCK, :], low)

    def sm(u, s):
        t, g, rows, cols = unit(u)
        m_prev = band_prev_first if t == 0 else band_prev
        m_next = band_next_last if t == nt - 1 else band_next
        s = jnp.concatenate([s[:, :BLOCK] + m_prev, s[:, BLOCK:2 * BLOCK], s[:, 2 * BLOCK:] + m_next], axis=1)
        sink = jnp.where(is_low_head, sink_ref[2 * g], sink_ref[2 * g + 1])
        return _softmax_numerator(s, sink)

    def pv(u, e):
        t, g, rows, cols = unit(u)
        o_ref[rows, cols] = _pv_two_heads(*e, vwin[t * BLOCK:(t + 3) * BLOCK, :], low)

    _software_pipeline(nt * ng, (qk, sm, pv), WIN_PIPE_LAGS)


def _window_call(aq, ak, av, sink, band):
    bsz, seq, _ = aq.shape
    nt = TQ_WIN // BLOCK
    nb = seq // BLOCK
    assert nt >= 2
    cur = lambda w: pl.BlockSpec((None, TQ_WIN, w), lambda b, i: (b, i, 0))
    prev = pl.BlockSpec((None, BLOCK, A_KV), lambda b, i: (b, jnp.maximum(i * nt - 1, 0), 0))
    nxt = pl.BlockSpec((None, BLOCK, A_KV), lambda b, i: (b, jnp.minimum(i * nt + nt, nb - 1), 0))
    kwin = pltpu.VMEM((TQ_WIN + 2 * BLOCK, A_KV), jnp.bfloat16)
    vwin = pltpu.VMEM((TQ_WIN + 2 * BLOCK, A_KV + LANES_V7X), jnp.bfloat16)
    return pl.pallas_call(
        _window_kernel,
        grid=(bsz, seq // TQ_WIN),
        in_specs=[
            pl.BlockSpec(memory_space=pltpu.SMEM),
            pl.BlockSpec(band.shape, lambda b, i: (0, 0)),
            cur(A_Q), prev, cur(A_KV), nxt, prev, cur(A_KV), nxt,
        ],
        out_specs=cur(A_Q),
        out_shape=jax.ShapeDtypeStruct((bsz, seq, A_Q), jnp.bfloat16),
        scratch_shapes=[kwin, vwin],
        compiler_params=_cparams(2),
        name="window_attn",
    )(sink, band, aq, ak, ak, ak, av, av, av)


def _nbr_kernel(rows, bias_ref, q_ref, *refs):
    nwin = NBR_WIN_BLOCKS
    k_blocks, v_blocks = refs[:nwin], refs[nwin:2 * nwin]
    o_ref, kwin, vwin = refs[2 * nwin:]
    i = pl.program_id(1)
    blk = KV_ROWS_BLK * GRID_W
    ng = C_W // LANES_V7X
    vw = 2 * LANES_V7X
    for n in range(nwin):
        kwin[n * blk:(n + 1) * blk, :] = k_blocks[n][...]
        for g in range(ng):
            vwin[n * blk:(n + 1) * blk, g * vw:g * vw + LANES_V7X] = (
                v_blocks[n][:, g * LANES_V7X:(g + 1) * LANES_V7X])
    for g in range(ng):
        vwin[:, g * vw + LANES_V7X:(g + 1) * vw] = jnp.ones((nwin * blk, LANES_V7X), jnp.bfloat16)
    r0 = i * TR_NBR
    win_start = _nbr_window_block(i, rows) * KV_ROWS_BLK
    low = _lane_is_low((GRID_W, LANES_V7X))
    nkeys = NA_KH * GRID_W
    key_off, bias_base = [], []
    for t in range(TR_NBR):
        r = r0 + t
        rs = jnp.clip(r - NA_KH // 2, 0, rows - NA_KH)
        key_off.append(pl.multiple_of((rs - win_start) * GRID_W, GRID_W))
        bias_base.append(NA_KH - 1 - (r - rs))

    def unit(u):
        t, g = divmod(u, ng)
        return t, g, slice(t * GRID_W, (t + 1) * GRID_W), slice(g * LANES_V7X, (g + 1) * LANES_V7X)

    def qk(u):
        t, g, qrows, cols = unit(u)
        return _qk_two_heads(q_ref[qrows, cols], kwin[pl.ds(key_off[t], nkeys), cols], low)

    def sm(u, s):
        t, g, qrows, cols = unit(u)
        bias = jnp.concatenate([bias_ref[g, bias_base[t] + 2 * m] for m in range(NA_KH // 2)], axis=1)
        return _softmax_numerator(s + bias)

    def pv(u, e):
        t, g, qrows, cols = unit(u)
        o_ref[qrows, cols] = _pv_two_heads(*e, vwin[pl.ds(key_off[t], nkeys), g * vw:(g + 1) * vw], low)

    _software_pipeline(TR_NBR * ng, (qk, sm, pv), NBR_PIPE_LAGS)


def _nbr_window_block(i, rows):
    per_tile = TR_NBR // KV_ROWS_BLK
    nblk = rows // KV_ROWS_BLK
    return jnp.clip(i * per_tile - 1, 0, nblk - (per_tile + 2))


def _nbr_call(cq, ck, cv, bias_tab):
    bsz, seq, _ = cq.shape
    rows = seq // GRID_W
    tq = TR_NBR * GRID_W
    blk = KV_ROWS_BLK * GRID_W
    nwin = NBR_WIN_BLOCKS
    assert rows % TR_NBR == 0 and rows >= nwin * KV_ROWS_BLK and NA_KH // 2 <= KV_ROWS_BLK
    kv_spec = lambda n: pl.BlockSpec((None, blk, C_W), lambda b, i: (b, _nbr_window_block(i, rows) + n, 0))
    q_spec = pl.BlockSpec((None, tq, C_W), lambda b, i: (b, i, 0))
    kv_specs = [kv_spec(n) for n in range(nwin)]
    return pl.pallas_call(
        partial(_nbr_kernel, rows),
        grid=(bsz, rows // TR_NBR),
        in_specs=[pl.BlockSpec(bias_tab.shape, lambda b, i: (0, 0, 0, 0)), q_spec] + kv_specs + kv_specs,
        out_specs=q_spec,
        out_shape=jax.ShapeDtypeStruct((bsz, seq, C_W), jnp.bfloat16),
        scratch_shapes=[pltpu.VMEM((nwin * blk, C_W), jnp.bfloat16),
                        pltpu.VMEM((nwin * blk, 2 * C_W), jnp.bfloat16)],
        compiler_params=_cparams(2),
        name="nbr_attn",
    )(bias_tab, cq, *([ck] * nwin), *([cv] * nwin))


def _ffn_kernel(x_ref, oa_ref, ob_ref, oc_ref, mod_ref, g2_ref, wo_ref, w1_ref, w2_ref, y_ref):
    gate1 = mod_ref[2:3, :]
    shift2 = mod_ref[3:4, :]
    scale2 = mod_ref[4:5, :]
    gate2 = mod_ref[5:6, :]
    mix_in = jnp.concatenate([oa_ref[...], ob_ref[...], oc_ref[...]], axis=1)
    mix = jnp.dot(mix_in, wo_ref[...], preferred_element_type=jnp.float32)
    x1 = x_ref[...] + gate1 * mix
    ms = jnp.mean(x1 * x1, axis=-1, keepdims=True)
    h2 = (((x1 * lax.rsqrt(ms + EPS)) * g2_ref[...]) * (1.0 + scale2) + shift2).astype(jnp.bfloat16)
    ff = None
    for j in range(D_FF // FF_CHUNK):
        cols = slice(j * FF_CHUNK, (j + 1) * FF_CHUNK)
        a = jnp.maximum(jnp.dot(h2, w1_ref[:, cols], preferred_element_type=jnp.float32), 0.0)
        part = jnp.dot((a * a).astype(jnp.bfloat16), w2_ref[cols, :], preferred_element_type=jnp.float32)
        ff = part if ff is None else ff + part
    y_ref[...] = x1 + gate2 * ff


def _ffn_call(x, oa, ob, oc, mod, layer, boff, p):
    bsz, seq, _ = x.shape
    tm = TM_FFN
    tok = lambda w: pl.BlockSpec((None, tm, w), lambda b, i: (b, i, 0))
    const2 = lambda a: pl.BlockSpec(a.shape, lambda b, i: (0, 0), pipeline_mode=pl.Buffered(1))
    return pl.pallas_call(
        _ffn_kernel,
        grid=(bsz, seq // tm),
        in_specs=[
            tok(D_MODEL), tok(A_Q), tok(B_W), tok(C_W),
            pl.BlockSpec((None, None, 6, D_MODEL), lambda b, i: (layer, boff + b, 0, 0)),
            const2(p["g2"]), const2(p["w_out"]), const2(p["w_ff1"]), const2(p["w_ff2"]),
        ],
        out_specs=tok(D_MODEL),
        out_shape=jax.ShapeDtypeStruct((bsz, seq, D_MODEL), jnp.float32),
        compiler_params=_cparams(2),
        name="outproj_ffn",
    )(x, oa, ob, oc, mod, p["g2"], p["w_out"], p["w_ff1"], p["w_ff2"])


def _tile_heads(g, n_heads, scale=1.0):
    return (jnp.tile(g.astype(jnp.float32), n_heads) * scale).reshape(1, n_heads * HEAD_DIM)


def _rope_tables(seq):
    half = ROPE_DIMS // 2
    inv = ROPE_THETA ** (-jnp.arange(0, ROPE_DIMS, 2, dtype=jnp.float32) / ROPE_DIMS)
    ang = jnp.arange(seq, dtype=jnp.float32)[:, None] * inv[None, :]
    cos, sin = jnp.cos(ang), jnp.sin(ang)
    one = jnp.ones((seq, HEAD_DIM - ROPE_DIMS), jnp.float32)
    zero_t = jnp.zeros((seq, HEAD_DIM - ROPE_DIMS), jnp.float32)
    zero_h = jnp.zeros((seq, half), jnp.float32)
    rep = LANES_V7X // HEAD_DIM
    cos_t = jnp.tile(jnp.concatenate([cos, cos, one], axis=1), (1, rep))
    sin_up = jnp.tile(jnp.concatenate([-sin, zero_h, zero_t], axis=1), (1, rep))
    sin_dn = jnp.tile(jnp.concatenate([zero_h, sin, zero_t], axis=1), (1, rep))
    return cos_t, sin_up, sin_dn


def _band_mask():
    qi = np.arange(BLOCK)[:, None]
    kj = np.arange(3 * BLOCK)[None, :]
    ok = np.abs(kj - BLOCK - qi) <= BLOCK
    band = np.where(ok, 0.0, NEG_INF).astype(np.float32)
    return jnp.asarray(np.concatenate([band, band], axis=0))


def _nbr_bias_table(rel_bias):
    col = np.arange(GRID_W)
    cs = np.clip(col - NA_KW // 2, 0, GRID_W - NA_KW)
    col_in = (col[None, :] >= cs[:, None]) & (col[None, :] < cs[:, None] + NA_KW)
    dc = np.clip(col[None, :] - col[:, None] + NA_KW - 1, 0, 2 * NA_KW - 2)
    ndc = 2 * NA_KW - 1
    onehot = (dc.reshape(-1)[None, :] == np.arange(ndc)[:, None]).astype(np.float32)
    tm = jnp.einsum("hrd,dx->hrx", rel_bias.astype(jnp.float32), jnp.asarray(onehot),
                    precision=lax.Precision.HIGHEST)
    tm = jnp.where(jnp.asarray(col_in.reshape(-1))[None, None, :], tm * LOG2E, NEG_INF)
    tm = tm.reshape(C_HEADS, 2 * NA_KH - 1, GRID_W, GRID_W)
    pair = jnp.concatenate([tm[:, :-1], tm[:, 1:]], axis=-1)
    pair = pair.reshape(C_HEADS // 2, 2, 2 * NA_KH - 2, GRID_W, 2 * GRID_W)
    return pair.transpose(0, 2, 1, 3, 4).reshape(C_HEADS // 2, 2 * NA_KH - 2, 2 * GRID_W, 2 * GRID_W)


def _layer_params(l, seq_tables, norm1_g, norm2_g, w_in, qa_norm_g, ka_norm_g, a_sink, bv_norm_g,
                  w_spatial, b_spatial, qc_norm_g, kc_norm_g, c_rel_bias, w_out, w_ff1, w_ff2):
    order = np.asarray(A_HEAD_ORDER)
    q_cols = (order[:, None] * HEAD_DIM + np.arange(HEAD_DIM)[None, :]).reshape(-1)
    in_cols = np.concatenate([q_cols, np.arange(A_Q, D_IN)])
    out_rows = np.concatenate([q_cols, np.arange(A_Q, D_MIX)])
    seg = np.kron(np.eye(SEG_W // HEAD_DIM), np.ones((HEAD_DIM, HEAD_DIM))).astype(np.float32)
    cos_t, sin_up, sin_dn = seq_tables
    ws = w_spatial[l].astype(jnp.bfloat16)
    return {
        "g1": norm1_g[l].reshape(1, D_MODEL),
        "g2": norm2_g[l].reshape(1, D_MODEL),
        "w_in": w_in[l][:, in_cols].astype(jnp.bfloat16),
        "w_out": w_out[l][out_rows, :].astype(jnp.bfloat16),
        "w_ff1": w_ff1[l].astype(jnp.bfloat16),
        "w_ff2": w_ff2[l].astype(jnp.bfloat16),
        "seg": jnp.asarray(seg, jnp.bfloat16),
        "ga": jnp.concatenate([_tile_heads(qa_norm_g[l], A_HEADS, ATTN_SCALE * LOG2E),
                               _tile_heads(ka_norm_g[l], A_KV_HEADS)], axis=1),
        "gc": jnp.concatenate([_tile_heads(qc_norm_g[l], C_HEADS, ATTN_SCALE * LOG2E),
                               _tile_heads(kc_norm_g[l], C_HEADS)], axis=1),
        "gbv": bv_norm_g[l].reshape(1, B_W),
        "cos": cos_t, "sin_up": sin_up, "sin_dn": sin_dn,
        "ws": jnp.concatenate([ws[0::2], ws[1::2]], axis=2),
        "bs": jnp.repeat(b_spatial[l].T, HEAD_DIM, axis=1),
        "sink": a_sink[l][order].astype(jnp.float32) * LOG2E,
        "nbr_bias": _nbr_bias_table(c_rel_bias[l]),
    }


def _trunk(x, mod, boff, layer_params):
    for l, p in enumerate(layer_params):
        aq, ak, av, ob, cq, ck, cv = _inproj_call(x, mod, l, boff, p)
        oa = _window_call(aq, ak, av, p["sink"], p["band"])
        oc = _nbr_call(cq, ck, cv, p["nbr_bias"])
        x = _ffn_call(x, oa, ob, oc, mod, l, boff, p)
    return x


def kernel(x_prompt, x_sample, c_prompt, c_sample, norm1_g, norm2_g, w_ada, b_ada, w_in, qa_norm_g, ka_norm_g, a_sink, bv_norm_g, w_spatial, b_spatial, qc_norm_g, kc_norm_g, c_rel_bias, w_out, w_ff1, w_ff2):
    depth = w_in.shape[0]
    nb_prompt, seq_prompt, _ = x_prompt.shape
    nb_sample, seq_sample, _ = x_sample.shape
    assert nb_prompt + nb_sample <= 8
    for seq in (seq_prompt, seq_sample):
        assert seq % TM_IN == 0 and seq % TM_FFN == 0 and seq % TQ_WIN == 0 and seq // BLOCK >= 3

    c8 = jnp.zeros((8, D_MODEL), jnp.float32)
    c8 = c8.at[:nb_prompt].set(c_prompt).at[nb_prompt:nb_prompt + nb_sample].set(c_sample)
    mod = _ada_call(c8, w_ada, b_ada).reshape(depth, 8, 6, D_MODEL)

    seq_tables = _rope_tables(max(seq_prompt, seq_sample))
    band = _band_mask()
    layer_params = []
    for l in range(depth):
        p = _layer_params(l, seq_tables, norm1_g, norm2_g, w_in, qa_norm_g, ka_norm_g, a_sink,
                          bv_norm_g, w_spatial, b_spatial, qc_norm_g, kc_norm_g, c_rel_bias,
                          w_out, w_ff1, w_ff2)
        p["band"] = band
        layer_params.append(p)

    y_prompt = _trunk(x_prompt, mod, 0, layer_params)
    y_sample = _trunk(x_sample, mod, nb_prompt, layer_params)
    return (y_prompt, y_sample)
```

```python
from functools import partial

import numpy as np
import jax
import jax.numpy as jnp
from jax import lax
from jax.experimental import pallas as pl
from jax.experimental.pallas import tpu as pltpu

D_MODEL = 1024
HEAD_DIM = 64
A_HEADS = 6
A_KV_HEADS = 2
A_GROUP = A_HEADS // A_KV_HEADS
B_GROUPS = 4
C_HEADS = 6
A_Q = A_HEADS * HEAD_DIM
A_KV = A_KV_HEADS * HEAD_DIM
B_W = B_GROUPS * HEAD_DIM
C_W = C_HEADS * HEAD_DIM
D_IN = A_Q + 2 * A_KV + 2 * B_W + 3 * C_W
D_MIX = A_Q + B_W + C_W
BLOCK = 128
CHUNK = 128
GRID_W = 64
NA_KH = 8
NA_KW = 16
ROPE_THETA = 500000.0
ROPE_DIMS = HEAD_DIM // 4
D_FF = 4 * D_MODEL
EPS = 1e-6
ATTN_SCALE = HEAD_DIM ** -0.5
LOG2E = float(np.log2(np.e))

LANES_V7X = 128
MXU_DIM_V7X = 256
SEG_W = MXU_DIM_V7X
VMEM_LIMIT_BYTES_V7X = 56 * 1024 * 1024

TM_IN = 1024
TM_FFN = 1024
FF_CHUNK = 1024
TQ_WIN = 2048
TR_NBR = 32
KV_ROWS_BLK = 4
NBR_WIN_BLOCKS = TR_NBR // KV_ROWS_BLK + 2
ADA_TN = 1536
WIN_PIPE_LAGS = (2, 2)
NBR_PIPE_LAGS = (2, 3)

NEG_INF = float("-inf")

A_HEAD_ORDER = (0, 3, 1, 4, 2, 5)


def _cparams(n_axes):
    return pltpu.CompilerParams(
        dimension_semantics=("arbitrary",) * n_axes,
        vmem_limit_bytes=VMEM_LIMIT_BYTES_V7X,
    )


def _lane_is_low(shape):
    lane = lax.broadcasted_iota(jnp.int32, shape, len(shape) - 1)
    return (lane % LANES_V7X) < HEAD_DIM


def _ada_kernel(c_ref, w_ref, b_ref, o_ref):
    c = c_ref[...]
    s = (c * jax.nn.sigmoid(c)).astype(jnp.bfloat16)
    w = w_ref[...].astype(jnp.bfloat16)
    o_ref[...] = jnp.dot(s, w, preferred_element_type=jnp.float32) + b_ref[...]


def _ada_call(c8, w_ada, b_ada):
    depth = w_ada.shape[0]
    n = w_ada.shape[2]
    return pl.pallas_call(
        _ada_kernel,
        grid=(depth, n // ADA_TN),
        in_specs=[
            pl.BlockSpec((8, D_MODEL), lambda l, j: (0, 0)),
            pl.BlockSpec((None, D_MODEL, ADA_TN), lambda l, j: (l, 0, j)),
            pl.BlockSpec((None, 1, ADA_TN), lambda l, j: (l, 0, j)),
        ],
        out_specs=pl.BlockSpec((None, 8, ADA_TN), lambda l, j: (l, 0, j)),
        out_shape=jax.ShapeDtypeStruct((depth, 8, n), jnp.float32),
        compiler_params=_cparams(2),
        name="ada_mod",
    )(c8, w_ada, b_ada.reshape(depth, 1, n))


def _head_rms(x, seg_ones, gain):
    outs = []
    for j in range(x.shape[1] // SEG_W):
        xj = x[:, j * SEG_W:(j + 1) * SEG_W]
        ssum = jnp.dot((xj * xj).astype(jnp.bfloat16), seg_ones, preferred_element_type=jnp.float32)
        outs.append((xj * lax.rsqrt(ssum * (1.0 / HEAD_DIM) + EPS)) * gain[:, j * SEG_W:(j + 1) * SEG_W])
    return jnp.concatenate(outs, axis=1)


def _rope(x, cos_t, sin_t):
    n = x.shape[1] // LANES_V7X
    lane = lax.broadcasted_iota(jnp.int32, (1, LANES_V7X), 1) % HEAD_DIM
    first_half = lane < ROPE_DIMS // 2
    outs = []
    for j in range(n):
        xj = x[:, j * LANES_V7X:(j + 1) * LANES_V7X]
        up = pltpu.roll(xj, LANES_V7X - ROPE_DIMS // 2, 1)
        dn = pltpu.roll(xj, ROPE_DIMS // 2, 1)
        outs.append(xj * cos_t + jnp.where(first_half, up, dn) * sin_t)
    return outs[0] if n == 1 else jnp.concatenate(outs, axis=1)


def _inproj_kernel(x_ref, mod_ref, g1_ref, w_ref, seg_ref, ga_ref, gc_ref,
                   gbv_ref, rope_ref, ws_ref, bs_ref,
                   aq_ref, ak_ref, av_ref, ob_ref, cq_ref, ck_ref, cv_ref):
    x = x_ref[...]
    shift1 = mod_ref[0:1, :]
    scale1 = mod_ref[1:2, :]
    ms = jnp.mean(x * x, axis=-1, keepdims=True)
    h = ((x * lax.rsqrt(ms + EPS)) * g1_ref[...]) * (1.0 + scale1) + shift1
    proj = jnp.dot(h.astype(jnp.bfloat16), w_ref[...], preferred_element_type=jnp.float32)

    seg = seg_ref[...]
    o_av = A_Q + A_KV
    o_bu = o_av + A_KV
    o_bv = o_bu + B_W
    o_cq = o_bv + B_W
    o_cv = o_cq + 2 * C_W

    a_n = _rope(_head_rms(proj[:, 0:o_av], seg, ga_ref[...]),
                rope_ref[:, :LANES_V7X], rope_ref[:, LANES_V7X:])
    aq_ref[...] = a_n[:, :A_Q].astype(jnp.bfloat16)
    ak_ref[...] = a_n[:, A_Q:].astype(jnp.bfloat16)
    av_ref[...] = proj[:, o_av:o_bu].astype(jnp.bfloat16)

    c_n = _head_rms(proj[:, o_cq:o_cv], seg, gc_ref[...])
    cq_ref[...] = c_n[:, :C_W].astype(jnp.bfloat16)
    ck_ref[...] = c_n[:, C_W:].astype(jnp.bfloat16)
    cv_ref[...] = proj[:, o_cv:].astype(jnp.bfloat16)

    u = jax.nn.gelu(proj[:, o_bu:o_bv])
    v = jax.nn.gelu(proj[:, o_bv:o_cq])
    vms = jnp.mean(v * v, axis=-1, keepdims=True)
    vr = ((v * lax.rsqrt(vms + EPS)) * gbv_ref[...]).astype(jnp.bfloat16)
    low = _lane_is_low((CHUNK, LANES_V7X))
    zero = jnp.zeros((), jnp.bfloat16)
    tm = x.shape[0]
    for c in range(tm // CHUNK):
        rows = slice(c * CHUNK, (c + 1) * CHUNK)
        sv_parts = []
        for j in range(B_W // LANES_V7X):
            vj = vr[rows, j * LANES_V7X:(j + 1) * LANES_V7X]
            v2 = jnp.concatenate([jnp.where(low, vj, zero), jnp.where(low, zero, vj)], axis=0)
            sv_parts.append(jnp.dot(ws_ref[j], v2, preferred_element_type=jnp.float32))
        sv = jnp.concatenate(sv_parts, axis=1) + bs_ref[...]
        ob_ref[rows, :] = (u[rows, :] * sv).astype(jnp.bfloat16)


def _inproj_call(x, mod, layer, boff, p):
    bsz, seq, _ = x.shape
    tm = TM_IN
    tok = lambda w: pl.BlockSpec((None, tm, w), lambda b, i: (b, i, 0))
    const2 = lambda a: pl.BlockSpec(a.shape, lambda b, i: (0, 0))
    bf = lambda w: jax.ShapeDtypeStruct((bsz, seq, w), jnp.bfloat16)
    rope_spec = pl.BlockSpec((tm, 2 * LANES_V7X), lambda b, i: (i, 0))
    return pl.pallas_call(
        _inproj_kernel,
        grid=(bsz, seq // tm),
        in_specs=[
            tok(D_MODEL),
            pl.BlockSpec((None, None, 6, D_MODEL), lambda b, i: (layer, boff + b, 0, 0)),
            const2(p["g1"]), const2(p["w_in"]), const2(p["seg"]),
            const2(p["ga"]), const2(p["gc"]), const2(p["gbv"]),
            rope_spec,
            pl.BlockSpec(p["ws"].shape, lambda b, i: (0, 0, 0)),
            const2(p["bs"]),
        ],
        out_specs=[tok(A_Q), tok(A_KV), tok(A_KV), tok(B_W), tok(C_W), tok(C_W), tok(C_W)],
        out_shape=[bf(A_Q), bf(A_KV), bf(A_KV), bf(B_W), bf(C_W), bf(C_W), bf(C_W)],
        compiler_params=_cparams(2),
        name="inproj",
    )(x, mod, p["g1"], p["w_in"], p["seg"], p["ga"], p["gc"], p["gbv"],
      p["rope"], p["ws"], p["bs"])


_NT_DIMS = (((1,), (1,)), ((), ()))


def _qk_two_heads(q, k, low):
    zero = jnp.zeros((), q.dtype)
    q2 = jnp.concatenate([jnp.where(low, q, zero), jnp.where(low, zero, q)], axis=0)
    return lax.dot_general(q2, k, _NT_DIMS, preferred_element_type=jnp.float32)


def _softmax_numerator(s, extra_logit=None):
    m = jnp.max(s, axis=-1, keepdims=True)
    if extra_logit is None:
        return jnp.exp2(s - m).astype(jnp.bfloat16), None
    m = jnp.maximum(m, extra_logit)
    return jnp.exp2(s - m).astype(jnp.bfloat16), jnp.exp2(extra_logit - m)


def _pv_two_heads(e2, extra, v_ones, low):
    r = jnp.dot(e2, v_ones, preferred_element_type=jnp.float32)
    l2 = r[:, LANES_V7X:]
    if extra is not None:
        l2 = l2 + extra
    o2 = r[:, :LANES_V7X] / l2
    n = o2.shape[0] // 2
    return jnp.where(low, o2[:n], o2[n:]).astype(jnp.bfloat16)


def _software_pipeline(n, stages, lags):
    offs = [0]
    for lag in lags:
        offs.append(offs[-1] + lag)
    vals = [dict() for _ in stages]
    for step in range(n + offs[-1]):
        for si, fn in enumerate(stages):
            u = step - offs[si]
            if 0 <= u < n:
                vals[si][u] = fn(u) if si == 0 else fn(u, vals[si - 1].pop(u))


def _window_kernel(sink_ref, band_ref, q_ref, kp_ref, kc_ref, kn_ref, vp_ref, vc_ref, vn_ref, o_ref,
                   kwin, vwin):
    i = pl.program_id(1)
    last = pl.num_programs(1) - 1
    kwin[0:BLOCK, :] = kp_ref[...]
    kwin[BLOCK:BLOCK + TQ_WIN, :] = kc_ref[...]
    kwin[BLOCK + TQ_WIN:, :] = kn_ref[...]
    vwin[0:BLOCK, :A_KV] = vp_ref[...]
    vwin[BLOCK:BLOCK + TQ_WIN, :A_KV] = vc_ref[...]
    vwin[BLOCK + TQ_WIN:, :A_KV] = vn_ref[...]
    vwin[:, A_KV:] = jnp.ones((TQ_WIN + 2 * BLOCK, LANES_V7X), jnp.bfloat16)
    band_prev = band_ref[:, :BLOCK]
    band_next = band_ref[:, 2 * BLOCK:]
    is_low_head = lax.broadcasted_iota(jnp.int32, (2 * BLOCK, 1), 0) < BLOCK
    low = _lane_is_low((BLOCK, LANES_V7X))
    nt = TQ_WIN // BLOCK
    ng = A_Q // LANES_V7X
    band_prev_first = band_prev + jnp.where(i == 0, NEG_INF, 0.0)
    band_next_last = band_next + jnp.where(i == last, NEG_INF, 0.0)

    def unit(u):
        t, g = divmod(u, ng)
        return t, g, slice(t * BLOCK, (t + 1) * BLOCK), slice(g * LANES_V7X, (g + 1) * LANES_V7X)

    def qk(u):
        t, g, rows, cols = unit(u)
        return _qk_two_heads(q_ref[rows, cols], kwin[t * BLOCK:(t + 3) * BLOCK, :], low)

    def sm(u, s):
        t, g, rows, cols = unit(u)
        m_prev = band_prev_first if t == 0 else band_prev
        m_next = band_next_last if t == nt - 1 else band_next
        s = jnp.concatenate([s[:, :BLOCK] + m_prev, s[:, BLOCK:2 * BLOCK], s[:, 2 * BLOCK:] + m_next], axis=1)
        sink = jnp.where(is_low_head, sink_ref[2 * g], sink_ref[2 * g + 1])
        return _softmax_numerator(s, sink)

    def pv(u, e):
        t, g, rows, cols = unit(u)
        o_ref[rows, cols] = _pv_two_heads(*e, vwin[t * BLOCK:(t + 3) * BLOCK, :], low)

    _software_pipeline(nt * ng, (qk, sm, pv), WIN_PIPE_LAGS)


def _window_call(aq, ak, av, sink, band):
    bsz, seq, _ = aq.shape
    nt = TQ_WIN // BLOCK
    nb = seq // BLOCK
    assert nt >= 2
    cur = lambda w: pl.BlockSpec((None, TQ_WIN, w), lambda b, i: (b, i, 0))
    prev = pl.BlockSpec((None, BLOCK, A_KV), lambda b, i: (b, jnp.maximum(i * nt - 1, 0), 0))
    nxt = pl.BlockSpec((None, BLOCK, A_KV), lambda b, i: (b, jnp.minimum(i * nt + nt, nb - 1), 0))
    kwin = pltpu.VMEM((TQ_WIN + 2 * BLOCK, A_KV), jnp.bfloat16)
    vwin = pltpu.VMEM((TQ_WIN + 2 * BLOCK, A_KV + LANES_V7X), jnp.bfloat16)
    return pl.pallas_call(
        _window_kernel,
        grid=(bsz, seq // TQ_WIN),
        in_specs=[
            pl.BlockSpec(memory_space=pltpu.SMEM),
            pl.BlockSpec(band.shape, lambda b, i: (0, 0)),
            cur(A_Q), prev, cur(A_KV), nxt, prev, cur(A_KV), nxt,
        ],
        out_specs=cur(A_Q),
        out_shape=jax.ShapeDtypeStruct((bsz, seq, A_Q), jnp.bfloat16),
        scratch_shapes=[kwin, vwin],
        compiler_params=_cparams(2),
        name="window_attn",
    )(sink, band, aq, ak, ak, ak, av, av, av)


def _nbr_kernel(rows, bias_ref, q_ref, *refs):
    nwin = NBR_WIN_BLOCKS
    k_blocks, v_blocks = refs[:nwin], refs[nwin:2 * nwin]
    o_ref, kwin, vwin = refs[2 * nwin:]
    i = pl.program_id(1)
    blk = KV_ROWS_BLK * GRID_W
    ng = C_W // LANES_V7X
    vw = 2 * LANES_V7X
    for n in range(nwin):
        kwin[n * blk:(n + 1) * blk, :] = k_blocks[n][...]
        for g in range(ng):
            vwin[n * blk:(n + 1) * blk, g * vw:g * vw + LANES_V7X] = (
                v_blocks[n][:, g * LANES_V7X:(g + 1) * LANES_V7X])
    for g in range(ng):
        vwin[:, g * vw + LANES_V7X:(g + 1) * vw] = jnp.ones((nwin * blk, LANES_V7X), jnp.bfloat16)
    r0 = i * TR_NBR
    win_start = _nbr_window_block(i, rows) * KV_ROWS_BLK
    low = _lane_is_low((GRID_W, LANES_V7X))
    nkeys = NA_KH * GRID_W
    key_off, bias_base = [], []
    for t in range(TR_NBR):
        r = r0 + t
        rs = jnp.clip(r - NA_KH // 2, 0, rows - NA_KH)
        key_off.append(pl.multiple_of((rs - win_start) * GRID_W, GRID_W))
        bias_base.append(NA_KH - 1 - (r - rs))

    def unit(u):
        t, g = divmod(u, ng)
        return t, g, slice(t * GRID_W, (t + 1) * GRID_W), slice(g * LANES_V7X, (g + 1) * LANES_V7X)

    def qk(u):
        t, g, qrows, cols = unit(u)
        return _qk_two_heads(q_ref[qrows, cols], kwin[pl.ds(key_off[t], nkeys), cols], low)

    def sm(u, s):
        t, g, qrows, cols = unit(u)
        bias = jnp.concatenate([bias_ref[g, bias_base[t] + 2 * m] for m in range(NA_KH // 2)], axis=1)
        return _softmax_numerator(s + bias)

    def pv(u, e):
        t, g, qrows, cols = unit(u)
        o_ref[qrows, cols] = _pv_two_heads(*e, vwin[pl.ds(key_off[t], nkeys), g * vw:(g + 1) * vw], low)

    _software_pipeline(TR_NBR * ng, (qk, sm, pv), NBR_PIPE_LAGS)


def _nbr_window_block(i, rows):
    per_tile = TR_NBR // KV_ROWS_BLK
    nblk = rows // KV_ROWS_BLK
    return jnp.clip(i * per_tile - 1, 0, nblk - (per_tile + 2))


def _nbr_call(cq, ck, cv, bias_tab):
    bsz, seq, _ = cq.shape
    rows = seq // GRID_W
    tq = TR_NBR * GRID_W
    blk = KV_ROWS_BLK * GRID_W
    nwin = NBR_WIN_BLOCKS
    assert rows % TR_NBR == 0 and rows >= nwin * KV_ROWS_BLK and NA_KH // 2 <= KV_ROWS_BLK
    kv_spec = lambda n: pl.BlockSpec((None, blk, C_W), lambda b, i: (b, _nbr_window_block(i, rows) + n, 0))
    q_spec = pl.BlockSpec((None, tq, C_W), lambda b, i: (b, i, 0))
    kv_specs = [kv_spec(n) for n in range(nwin)]
    return pl.pallas_call(
        partial(_nbr_kernel, rows),
        grid=(bsz, rows // TR_NBR),
        in_specs=[pl.BlockSpec(bias_tab.shape, lambda b, i: (0, 0, 0, 0)), q_spec] + kv_specs + kv_specs,
        out_specs=q_spec,
        out_shape=jax.ShapeDtypeStruct((bsz, seq, C_W), jnp.bfloat16),
        scratch_shapes=[pltpu.VMEM((nwin * blk, C_W), jnp.bfloat16),
                        pltpu.VMEM((nwin * blk, 2 * C_W), jnp.bfloat16)],
        compiler_params=_cparams(2),
        name="nbr_attn",
    )(bias_tab, cq, *([ck] * nwin), *([cv] * nwin))


def _ffn_kernel(x_ref, oa_ref, ob_ref, oc_ref, mod_ref, g2_ref, wo_ref, w1_ref, w2_ref, y_ref):
    gate1 = mod_ref[2:3, :]
    shift2 = mod_ref[3:4, :]
    scale2 = mod_ref[4:5, :]
    gate2 = mod_ref[5:6, :]
    mix_in = jnp.concatenate([oa_ref[...], ob_ref[...], oc_ref[...]], axis=1)
    mix = jnp.dot(mix_in, wo_ref[...], preferred_element_type=jnp.float32)
    x1 = x_ref[...] + gate1 * mix
    ms = jnp.mean(x1 * x1, axis=-1, keepdims=True)
    h2 = (((x1 * lax.rsqrt(ms + EPS)) * g2_ref[...]) * (1.0 + scale2) + shift2).astype(jnp.bfloat16)
    ff = None
    for j in range(D_FF // FF_CHUNK):
        cols = slice(j * FF_CHUNK, (j + 1) * FF_CHUNK)
        a = jnp.maximum(jnp.dot(h2, w1_ref[:, cols], preferred_element_type=jnp.float32), 0.0)
        part = jnp.dot((a * a).astype(jnp.bfloat16), w2_ref[cols, :], preferred_element_type=jnp.float32)
        ff = part if ff is None else ff + part
    y_ref[...] = x1 + gate2 * ff


def _ffn_call(x, oa, ob, oc, mod, layer, boff, p):
    bsz, seq, _ = x.shape
    tm = TM_FFN
    tok = lambda w: pl.BlockSpec((None, tm, w), lambda b, i: (b, i, 0))
    const2 = lambda a: pl.BlockSpec(a.shape, lambda b, i: (0, 0), pipeline_mode=pl.Buffered(1))
    return pl.pallas_call(
        _ffn_kernel,
        grid=(bsz, seq // tm),
        in_specs=[
            tok(D_MODEL), tok(A_Q), tok(B_W), tok(C_W),
            pl.BlockSpec((None, None, 6, D_MODEL), lambda b, i: (layer, boff + b, 0, 0)),
            const2(p["g2"]), const2(p["w_out"]), const2(p["w_ff1"]), const2(p["w_ff2"]),
        ],
        out_specs=tok(D_MODEL),
        out_shape=jax.ShapeDtypeStruct((bsz, seq, D_MODEL), jnp.float32),
        compiler_params=_cparams(2),
        name="outproj_ffn",
    )(x, oa, ob, oc, mod, p["g2"], p["w_out"], p["w_ff1"], p["w_ff2"])


def _tile_heads(g, n_heads, scale=1.0):
    return (jnp.tile(g.astype(jnp.float32), n_heads) * scale).reshape(1, n_heads * HEAD_DIM)


def _rope_tables(seq):
    inv = ROPE_THETA ** (-jnp.arange(0, ROPE_DIMS, 2, dtype=jnp.float32) / ROPE_DIMS)
    ang = jnp.arange(seq, dtype=jnp.float32)[:, None] * inv[None, :]
    cos, sin = jnp.cos(ang), jnp.sin(ang)
    one = jnp.ones((seq, HEAD_DIM - ROPE_DIMS), jnp.float32)
    zero_t = jnp.zeros((seq, HEAD_DIM - ROPE_DIMS), jnp.float32)
    rep = LANES_V7X // HEAD_DIM
    cos_h = jnp.concatenate([cos, cos, one], axis=1)
    sin_h = jnp.concatenate([-sin, sin, zero_t], axis=1)
    return jnp.concatenate([cos_h] * rep + [sin_h] * rep, axis=1)


def _band_mask():
    qi = np.arange(BLOCK)[:, None]
    kj = np.arange(3 * BLOCK)[None, :]
    ok = np.abs(kj - BLOCK - qi) <= BLOCK
    band = np.where(ok, 0.0, NEG_INF).astype(np.float32)
    return jnp.asarray(np.concatenate([band, band], axis=0))


def _nbr_bias_table(rel_bias):
    col = np.arange(GRID_W)
    cs = np.clip(col - NA_KW // 2, 0, GRID_W - NA_KW)
    col_in = (col[None, :] >= cs[:, None]) & (col[None, :] < cs[:, None] + NA_KW)
    dc = np.clip(col[None, :] - col[:, None] + NA_KW - 1, 0, 2 * NA_KW - 2)
    ndc = 2 * NA_KW - 1
    onehot = (dc.reshape(-1)[None, :] == np.arange(ndc)[:, None]).astype(np.float32)
    tm = jnp.einsum("hrd,dx->hrx", rel_bias.astype(jnp.float32), jnp.asarray(onehot),
                    precision=lax.Precision.HIGHEST)
    tm = jnp.where(jnp.asarray(col_in.reshape(-1))[None, None, :], tm * LOG2E, NEG_INF)
    tm = tm.reshape(C_HEADS, 2 * NA_KH - 1, GRID_W, GRID_W)
    pair = jnp.concatenate([tm[:, :-1], tm[:, 1:]], axis=-1)
    pair = pair.reshape(C_HEADS // 2, 2, 2 * NA_KH - 2, GRID_W, 2 * GRID_W)
    return pair.transpose(0, 2, 1, 3, 4).reshape(C_HEADS // 2, 2 * NA_KH - 2, 2 * GRID_W, 2 * GRID_W)


def _layer_params(l, seq_tables, norm1_g, norm2_g, w_in, qa_norm_g, ka_norm_g, a_sink, bv_norm_g,
                  w_spatial, b_spatial, qc_norm_g, kc_norm_g, c_rel_bias, w_out, w_ff1, w_ff2):
    order = np.asarray(A_HEAD_ORDER)
    q_cols = (order[:, None] * HEAD_DIM + np.arange(HEAD_DIM)[None, :]).reshape(-1)
    in_cols = np.concatenate([q_cols, np.arange(A_Q, D_IN)])
    out_rows = np.concatenate([q_cols, np.arange(A_Q, D_MIX)])
    seg = np.kron(np.eye(SEG_W // HEAD_DIM), np.ones((HEAD_DIM, HEAD_DIM))).astype(np.float32)
    ws = w_spatial[l].astype(jnp.bfloat16)
    return {
        "g1": norm1_g[l].reshape(1, D_MODEL),
        "g2": norm2_g[l].reshape(1, D_MODEL),
        "w_in": w_in[l][:, in_cols].astype(jnp.bfloat16),
        "w_out": w_out[l][out_rows, :].astype(jnp.bfloat16),
        "w_ff1": w_ff1[l].astype(jnp.bfloat16),
        "w_ff2": w_ff2[l].astype(jnp.bfloat16),
        "seg": jnp.asarray(seg, jnp.bfloat16),
        "ga": jnp.concatenate([_tile_heads(qa_norm_g[l], A_HEADS, ATTN_SCALE * LOG2E),
                               _tile_heads(ka_norm_g[l], A_KV_HEADS)], axis=1),
        "gc": jnp.concatenate([_tile_heads(qc_norm_g[l], C_HEADS, ATTN_SCALE * LOG2E),
                               _tile_heads(kc_norm_g[l], C_HEADS)], axis=1),
        "gbv": bv_norm_g[l].reshape(1, B_W),
        "rope": seq_tables,
        "ws": jnp.concatenate([ws[0::2], ws[1::2]], axis=2),
        "bs": jnp.repeat(b_spatial[l].T, HEAD_DIM, axis=1),
        "sink": a_sink[l][order].astype(jnp.float32) * LOG2E,
        "nbr_bias": _nbr_bias_table(c_rel_bias[l]),
    }


def _trunk(x, mod, boff, layer_params):
    for l, p in enumerate(layer_params):
        aq, ak, av, ob, cq, ck, cv = _inproj_call(x, mod, l, boff, p)
        oa = _window_call(aq, ak, av, p["sink"], p["band"])
        oc = _nbr_call(cq, ck, cv, p["nbr_bias"])
        x = _ffn_call(x, oa, ob, oc, mod, l, boff, p)
    return x


def kernel(x_prompt, x_sample, c_prompt, c_sample, norm1_g, norm2_g, w_ada, b_ada, w_in, qa_norm_g, ka_norm_g, a_sink, bv_norm_g, w_spatial, b_spatial, qc_norm_g, kc_norm_g, c_rel_bias, w_out, w_ff1, w_ff2):
    depth = w_in.shape[0]
    nb_prompt, seq_prompt, _ = x_prompt.shape
    nb_sample, seq_sample, _ = x_sample.shape
    assert nb_prompt + nb_sample <= 8
    for seq in (seq_prompt, seq_sample):
        assert seq % TM_IN == 0 and seq % TM_FFN == 0 and seq % TQ_WIN == 0 and seq // BLOCK >= 3

    c8 = jnp.zeros((8, D_MODEL), jnp.float32)
    c8 = c8.at[:nb_prompt].set(c_prompt).at[nb_prompt:nb_prompt + nb_sample].set(c_sample)
    mod = _ada_call(c8, w_ada, b_ada).reshape(depth, 8, 6, D_MODEL)

    seq_tables = _rope_tables(max(seq_prompt, seq_sample))
    band = _band_mask()
    layer_params = []
    for l in range(depth):
        p = _layer_params(l, seq_tables, norm1_g, norm2_g, w_in, qa_norm_g, ka_norm_g, a_sink,
                          bv_norm_g, w_spatial, b_spatial, qc_norm_g, kc_norm_g, c_rel_bias,
                          w_out, w_ff1, w_ff2)
        p["band"] = band
        layer_params.append(p)

    y_prompt = _trunk(x_prompt, mod, 0, layer_params)
    y_sample = _trunk(x_sample, mod, nb_prompt, layer_params)
    return (y_prompt, y_sample)
```

```python
from functools import partial

import numpy as np
import jax
import jax.numpy as jnp
from jax import lax
from jax.experimental import pallas as pl
from jax.experimental.pallas import tpu as pltpu

D_MODEL = 1024
HEAD_DIM = 64
A_HEADS = 6
A_KV_HEADS = 2
A_GROUP = A_HEADS // A_KV_HEADS
B_GROUPS = 4
C_HEADS = 6
A_Q = A_HEADS * HEAD_DIM
A_KV = A_KV_HEADS * HEAD_DIM
B_W = B_GROUPS * HEAD_DIM
C_W = C_HEADS * HEAD_DIM
D_IN = A_Q + 2 * A_KV + 2 * B_W + 3 * C_W
D_MIX = A_Q + B_W + C_W
BLOCK = 128
CHUNK = 128
GRID_W = 64
NA_KH = 8
NA_KW = 16
ROPE_THETA = 500000.0
ROPE_DIMS = HEAD_DIM // 4
D_FF = 4 * D_MODEL
EPS = 1e-6
ATTN_SCALE = HEAD_DIM ** -0.5
LOG2E = float(np.log2(np.e))

LANES_V7X = 128
MXU_DIM_V7X = 256
SEG_W = MXU_DIM_V7X
VMEM_LIMIT_BYTES_V7X = 56 * 1024 * 1024

TM_IN = 1024
TM_FFN = 1024
FF_CHUNK = 1024
TQ_WIN = 2048
TR_NBR = 32
KV_ROWS_BLK = 4
NBR_WIN_BLOCKS = TR_NBR // KV_ROWS_BLK + 2
ADA_TN = 1536
WIN_PIPE_LAGS = (2, 2)
NBR_PIPE_LAGS = (2, 3)

NEG_INF = float("-inf")

A_HEAD_ORDER = (0, 3, 1, 4, 2, 5)


def _cparams(n_axes):
    return pltpu.CompilerParams(
        dimension_semantics=("arbitrary",) * n_axes,
        vmem_limit_bytes=VMEM_LIMIT_BYTES_V7X,
    )


def _lane_is_low(shape):
    lane = lax.broadcasted_iota(jnp.int32, shape, len(shape) - 1)
    return (lane % LANES_V7X) < HEAD_DIM


def _ada_kernel(c_ref, w_ref, b_ref, o_ref):
    c = c_ref[...]
    s = (c * jax.nn.sigmoid(c)).astype(jnp.bfloat16)
    w = w_ref[...].astype(jnp.bfloat16)
    o_ref[...] = jnp.dot(s, w, preferred_element_type=jnp.float32) + b_ref[...]


def _ada_call(c8, w_ada, b_ada):
    depth = w_ada.shape[0]
    n = w_ada.shape[2]
    return pl.pallas_call(
        _ada_kernel,
        grid=(depth, n // ADA_TN),
        in_specs=[
            pl.BlockSpec((8, D_MODEL), lambda l, j: (0, 0)),
            pl.BlockSpec((None, D_MODEL, ADA_TN), lambda l, j: (l, 0, j)),
            pl.BlockSpec((None, 1, ADA_TN), lambda l, j: (l, 0, j)),
        ],
        out_specs=pl.BlockSpec((None, 8, ADA_TN), lambda l, j: (l, 0, j)),
        out_shape=jax.ShapeDtypeStruct((depth, 8, n), jnp.float32),
        compiler_params=_cparams(2),
        name="ada_mod",
    )(c8, w_ada, b_ada.reshape(depth, 1, n))


def _head_rms(x, seg_ones, gain):
    outs = []
    for j in range(x.shape[1] // SEG_W):
        xj = x[:, j * SEG_W:(j + 1) * SEG_W]
        ssum = jnp.dot((xj * xj).astype(jnp.bfloat16), seg_ones, preferred_element_type=jnp.float32)
        outs.append((xj * lax.rsqrt(ssum * (1.0 / HEAD_DIM) + EPS)) * gain[:, j * SEG_W:(j + 1) * SEG_W])
    return jnp.concatenate(outs, axis=1)


def _rope(x, cos_t, sin_t):
    n = x.shape[1] // LANES_V7X
    lane = lax.broadcasted_iota(jnp.int32, (1, LANES_V7X), 1) % HEAD_DIM
    first_half = lane < ROPE_DIMS // 2
    outs = []
    for j in range(n):
        xj = x[:, j * LANES_V7X:(j + 1) * LANES_V7X]
        up = pltpu.roll(xj, LANES_V7X - ROPE_DIMS // 2, 1)
        dn = pltpu.roll(xj, ROPE_DIMS // 2, 1)
        outs.append(xj * cos_t + jnp.where(first_half, up, dn) * sin_t)
    return outs[0] if n == 1 else jnp.concatenate(outs, axis=1)


def _inproj_kernel(x_ref, mod_ref, g1_ref, w_ref, seg_ref, ga_ref, gc_ref,
                   gbv_ref, rope_ref, ws_ref, bs_ref,
                   aq_ref, ak_ref, av_ref, ob_ref, cq_ref, ck_ref, cv_ref):
    x = x_ref[...]
    shift1 = mod_ref[0:1, :]
    scale1 = mod_ref[1:2, :]
    ms = jnp.mean(x * x, axis=-1, keepdims=True)
    h = ((x * lax.rsqrt(ms + EPS)) * g1_ref[...]) * (1.0 + scale1) + shift1
    proj = jnp.dot(h.astype(jnp.bfloat16), w_ref[...], preferred_element_type=jnp.float32)

    seg = seg_ref[...]
    o_av = A_Q + A_KV
    o_bu = o_av + A_KV
    o_bv = o_bu + B_W
    o_cq = o_bv + B_W
    o_cv = o_cq + 2 * C_W

    a_n = _rope(_head_rms(proj[:, 0:o_av], seg, ga_ref[...]),
                rope_ref[:, :LANES_V7X], rope_ref[:, LANES_V7X:])
    aq_ref[...] = a_n[:, :A_Q].astype(jnp.bfloat16)
    ak_ref[...] = a_n[:, A_Q:].astype(jnp.bfloat16)
    av_ref[...] = proj[:, o_av:o_bu].astype(jnp.bfloat16)

    c_n = _head_rms(proj[:, o_cq:o_cv], seg, gc_ref[...])
    cq_ref[...] = c_n[:, :C_W].astype(jnp.bfloat16)
    ck_ref[...] = c_n[:, C_W:].astype(jnp.bfloat16)
    cv_ref[...] = proj[:, o_cv:].astype(jnp.bfloat16)

    u = jax.nn.gelu(proj[:, o_bu:o_bv])
    v = jax.nn.gelu(proj[:, o_bv:o_cq])
    vms = jnp.mean(v * v, axis=-1, keepdims=True)
    vr = ((v * lax.rsqrt(vms + EPS)) * gbv_ref[...]).astype(jnp.bfloat16)
    low = _lane_is_low((CHUNK, LANES_V7X))
    zero = jnp.zeros((), jnp.bfloat16)
    tm = x.shape[0]
    for c in range(tm // CHUNK):
        rows = slice(c * CHUNK, (c + 1) * CHUNK)
        sv_parts = []
        for j in range(B_W // LANES_V7X):
            vj = vr[rows, j * LANES_V7X:(j + 1) * LANES_V7X]
            v2 = jnp.concatenate([jnp.where(low, vj, zero), jnp.where(low, zero, vj)], axis=0)
            sv_parts.append(jnp.dot(ws_ref[j], v2, preferred_element_type=jnp.float32))
        sv = jnp.concatenate(sv_parts, axis=1) + bs_ref[...]
        ob_ref[rows, :] = (u[rows, :] * sv).astype(jnp.bfloat16)


def _inproj_call(x, mod, layer, boff, p):
    bsz, seq, _ = x.shape
    tm = TM_IN
    tok = lambda w: pl.BlockSpec((None, tm, w), lambda b, i: (b, i, 0))
    const2 = lambda a: pl.BlockSpec(a.shape, lambda b, i: (0, 0))
    bf = lambda w: jax.ShapeDtypeStruct((bsz, seq, w), jnp.bfloat16)
    rope_spec = pl.BlockSpec((tm, 2 * LANES_V7X), lambda b, i: (i, 0))
    return pl.pallas_call(
        _inproj_kernel,
        grid=(bsz, seq // tm),
        in_specs=[
            tok(D_MODEL),
            pl.BlockSpec((None, None, 6, D_MODEL), lambda b, i: (layer, boff + b, 0, 0)),
            const2(p["g1"]), const2(p["w_in"]), const2(p["seg"]),
            const2(p["ga"]), const2(p["gc"]), const2(p["gbv"]),
            rope_spec,
            pl.BlockSpec(p["ws"].shape, lambda b, i: (0, 0, 0)),
            const2(p["bs"]),
        ],
        out_specs=[tok(A_Q), tok(A_KV), tok(A_KV), tok(B_W), tok(C_W), tok(C_W), tok(C_W)],
        out_shape=[bf(A_Q), bf(A_KV), bf(A_KV), bf(B_W), bf(C_W), bf(C_W), bf(C_W)],
        compiler_params=_cparams(2),
        name="inproj",
    )(x, mod, p["g1"], p["w_in"], p["seg"], p["ga"], p["gc"], p["gbv"],
      p["rope"], p["ws"], p["bs"])


_NT_DIMS = (((1,), (1,)), ((), ()))


def _qk_two_heads(q, k, low):
    zero = jnp.zeros((), q.dtype)
    q2 = jnp.concatenate([jnp.where(low, q, zero), jnp.where(low, zero, q)], axis=0)
    return lax.dot_general(q2, k, _NT_DIMS, preferred_element_type=jnp.float32)


def _softmax_numerator(s, extra_logit=None):
    m = jnp.max(s, axis=-1, keepdims=True)
    if extra_logit is None:
        return jnp.exp2(s - m).astype(jnp.bfloat16), None
    m = jnp.maximum(m, extra_logit)
    return jnp.exp2(s - m).astype(jnp.bfloat16), jnp.exp2(extra_logit - m)


def _pv_two_heads(e2, extra, v_ones, low):
    r = jnp.dot(e2, v_ones, preferred_element_type=jnp.float32)
    l2 = r[:, LANES_V7X:]
    if extra is not None:
        l2 = l2 + extra
    o2 = r[:, :LANES_V7X] / l2
    n = o2.shape[0] // 2
    return jnp.where(low, o2[:n], o2[n:]).astype(jnp.bfloat16)


def _software_pipeline(n, stages, lags):
    offs = [0]
    for lag in lags:
        offs.append(offs[-1] + lag)
    vals = [dict() for _ in stages]
    for step in range(n + offs[-1]):
        for si, fn in enumerate(stages):
            u = step - offs[si]
            if 0 <= u < n:
                vals[si][u] = fn(u) if si == 0 else fn(u, vals[si - 1].pop(u))


def _window_kernel(sink_ref, band_ref, q_ref, kp_ref, kc_ref, kn_ref, vp_ref, vc_ref, vn_ref, o_ref,
                   kwin, vwin):
    i = pl.program_id(1)
    last = pl.num_programs(1) - 1
    kwin[0:BLOCK, :] = kp_ref[...]
    kwin[BLOCK:BLOCK + TQ_WIN, :] = kc_ref[...]
    kwin[BLOCK + TQ_WIN:, :] = kn_ref[...]
    vwin[0:BLOCK, :A_KV] = vp_ref[...]
    vwin[BLOCK:BLOCK + TQ_WIN, :A_KV] = vc_ref[...]
    vwin[BLOCK + TQ_WIN:, :A_KV] = vn_ref[...]
    vwin[:, A_KV:] = jnp.ones((TQ_WIN + 2 * BLOCK, LANES_V7X), jnp.bfloat16)
    band_prev = band_ref[:, :BLOCK]
    band_next = band_ref[:, 2 * BLOCK:]
    is_low_head = lax.broadcasted_iota(jnp.int32, (2 * BLOCK, 1), 0) < BLOCK
    low = _lane_is_low((BLOCK, LANES_V7X))
    nt = TQ_WIN // BLOCK
    ng = A_Q // LANES_V7X
    band_prev_first = band_prev + jnp.where(i == 0, NEG_INF, 0.0)
    band_next_last = band_next + jnp.where(i == last, NEG_INF, 0.0)

    def unit(u):
        t, g = divmod(u, ng)
        return t, g, slice(t * BLOCK, (t + 1) * BLOCK), slice(g * LANES_V7X, (g + 1) * LANES_V7X)

    def qk(u):
        t, g, rows, cols = unit(u)
        return _qk_two_heads(q_ref[rows, cols], kwin[t * BLOCK:(t + 3) * BLOCK, :], low)

    def sm(u, s):
        t, g, rows, cols = unit(u)
        m_prev = band_prev_first if t == 0 else band_prev
        m_next = band_next_last if t == nt - 1 else band_next
        s = jnp.concatenate([s[:, :BLOCK] + m_prev, s[:, BLOCK:2 * BLOCK], s[:, 2 * BLOCK:] + m_next], axis=1)
        sink = jnp.where(is_low_head, sink_ref[2 * g], sink_ref[2 * g + 1])
        return _softmax_numerator(s, sink)

    def pv(u, e):
        t, g, rows, cols = unit(u)
        o_ref[rows, cols] = _pv_two_heads(*e, vwin[t * BLOCK:(t + 3) * BLOCK, :], low)

    _software_pipeline(nt * ng, (qk, sm, pv), WIN_PIPE_LAGS)


def _window_call(aq, ak, av, sink, band):
    bsz, seq, _ = aq.shape
    nt = TQ_WIN // BLOCK
    nb = seq // BLOCK
    assert nt >= 2
    cur = lambda w: pl.BlockSpec((None, TQ_WIN, w), lambda b, i: (b, i, 0))
    prev = pl.BlockSpec((None, BLOCK, A_KV), lambda b, i: (b, jnp.maximum(i * nt - 1, 0), 0))
    nxt = pl.BlockSpec((None, BLOCK, A_KV), lambda b, i: (b, jnp.minimum(i * nt + nt, nb - 1), 0))
    kwin = pltpu.VMEM((TQ_WIN + 2 * BLOCK, A_KV), jnp.bfloat16)
    vwin = pltpu.VMEM((TQ_WIN + 2 * BLOCK, A_KV + LANES_V7X), jnp.bfloat16)
    return pl.pallas_call(
        _window_kernel,
        grid=(bsz, seq // TQ_WIN),
        in_specs=[
            pl.BlockSpec(memory_space=pltpu.SMEM),
            pl.BlockSpec(band.shape, lambda b, i: (0, 0)),
            cur(A_Q), prev, cur(A_KV), nxt, prev, cur(A_KV), nxt,
        ],
        out_specs=cur(A_Q),
        out_shape=jax.ShapeDtypeStruct((bsz, seq, A_Q), jnp.bfloat16),
        scratch_shapes=[kwin, vwin],
        compiler_params=_cparams(2),
        name="window_attn",
    )(sink, band, aq, ak, ak, ak, av, av, av)


def _nbr_kernel(rows, bias_ref, q_ref, *refs):
    nwin = NBR_WIN_BLOCKS
    k_blocks, v_blocks = refs[:nwin], refs[nwin:2 * nwin]
    o_ref, kwin, vwin = refs[2 * nwin:]
    i = pl.program_id(1)
    blk = KV_ROWS_BLK * GRID_W
    ng = C_W // LANES_V7X
    vw = 2 * LANES_V7X
    for n in range(nwin):
        kwin[n * blk:(n + 1) * blk, :] = k_blocks[n][...]
        for g in range(ng):
            vwin[n * blk:(n + 1) * blk, g * vw:g * vw + LANES_V7X] = (
                v_blocks[n][:, g * LANES_V7X:(g + 1) * LANES_V7X])
    for g in range(ng):
        vwin[:, g * vw + LANES_V7X:(g + 1) * vw] = jnp.ones((nwin * blk, LANES_V7X), jnp.bfloat16)
    r0 = i * TR_NBR
    win_start = _nbr_window_block(i, rows) * KV_ROWS_BLK
    low = _lane_is_low((GRID_W, LANES_V7X))
    nkeys = NA_KH * GRID_W
    key_off, bias_base = [], []
    for t in range(TR_NBR):
        r = r0 + t
        rs = jnp.clip(r - NA_KH // 2, 0, rows - NA_KH)
        key_off.append(pl.multiple_of((rs - win_start) * GRID_W, GRID_W))
        bias_base.append(NA_KH - 1 - (r - rs))

    def unit(u):
        t, g = divmod(u, ng)
        return t, g, slice(t * GRID_W, (t + 1) * GRID_W), slice(g * LANES_V7X, (g + 1) * LANES_V7X)

    def qk(u):
        t, g, qrows, cols = unit(u)
        return _qk_two_heads(q_ref[qrows, cols], kwin[pl.ds(key_off[t], nkeys), cols], low)

    def sm(u, s):
        t, g, qrows, cols = unit(u)
        bias = jnp.concatenate([bias_ref[g, bias_base[t] + 2 * m] for m in range(NA_KH // 2)], axis=1)
        return _softmax_numerator(s + bias)

    def pv(u, e):
        t, g, qrows, cols = unit(u)
        o_ref[qrows, cols] = _pv_two_heads(*e, vwin[pl.ds(key_off[t], nkeys), g * vw:(g + 1) * vw], low)

    _software_pipeline(TR_NBR * ng, (qk, sm, pv), NBR_PIPE_LAGS)


def _nbr_window_block(i, rows):
    per_tile = TR_NBR // KV_ROWS_BLK
    nblk = rows // KV_ROWS_BLK
    return jnp.clip(i * per_tile - 1, 0, nblk - (per_tile + 2))


def _nbr_call(cq, ck, cv, bias_tab):
    bsz, seq, _ = cq.shape
    rows = seq // GRID_W
    tq = TR_NBR * GRID_W
    blk = KV_ROWS_BLK * GRID_W
    nwin = NBR_WIN_BLOCKS
    assert rows % TR_NBR == 0 and rows >= nwin * KV_ROWS_BLK and NA_KH // 2 <= KV_ROWS_BLK
    kv_spec = lambda n: pl.BlockSpec((None, blk, C_W), lambda b, i: (b, _nbr_window_block(i, rows) + n, 0))
    q_spec = pl.BlockSpec((None, tq, C_W), lambda b, i: (b, i, 0))
    kv_specs = [kv_spec(n) for n in range(nwin)]
    return pl.pallas_call(
        partial(_nbr_kernel, rows),
        grid=(bsz, rows // TR_NBR),
        in_specs=[pl.BlockSpec(bias_tab.shape, lambda b, i: (0, 0, 0, 0)), q_spec] + kv_specs + kv_specs,
        out_specs=q_spec,
        out_shape=jax.ShapeDtypeStruct((bsz, seq, C_W), jnp.bfloat16),
        scratch_shapes=[pltpu.VMEM((nwin * blk, C_W), jnp.bfloat16),
                        pltpu.VMEM((nwin * blk, 2 * C_W), jnp.bfloat16)],
        compiler_params=_cparams(2),
        name="nbr_attn",
    )(bias_tab, cq, *([ck] * nwin), *([cv] * nwin))


def _ffn_kernel(x_ref, oa_ref, ob_ref, oc_ref, mod_ref, g2_ref, wo_ref, w1_ref, w2_ref, y_ref):
    gate1 = mod_ref[2:3, :]
    shift2 = mod_ref[3:4, :]
    scale2 = mod_ref[4:5, :]
    gate2 = mod_ref[5:6, :]
    mix_in = jnp.concatenate([oa_ref[...], ob_ref[...], oc_ref[...]], axis=1)
    mix = jnp.dot(mix_in, wo_ref[...], preferred_element_type=jnp.float32)
    x1 = x_ref[...] + gate1 * mix
    ms = jnp.mean(x1 * x1, axis=-1, keepdims=True)
    h2 = (((x1 * lax.rsqrt(ms + EPS)) * g2_ref[...]) * (1.0 + scale2) + shift2).astype(jnp.bfloat16)
    ff = None
    for j in range(D_FF // FF_CHUNK):
        cols = slice(j * FF_CHUNK, (j + 1) * FF_CHUNK)
        a = jnp.maximum(jnp.dot(h2, w1_ref[:, cols], preferred_element_type=jnp.float32), 0.0)
        part = jnp.dot((a * a).astype(jnp.bfloat16), w2_ref[cols, :], preferred_element_type=jnp.float32)
        ff = part if ff is None else ff + part
    y_ref[...] = x1 + gate2 * ff


def _ffn_call(x, oa, ob, oc, mod, layer, boff, p):
    bsz, seq, _ = x.shape
    tm = TM_FFN
    tok = lambda w: pl.BlockSpec((None, tm, w), lambda b, i: (b, i, 0))
    const2 = lambda a: pl.BlockSpec(a.shape, lambda b, i: (0, 0), pipeline_mode=pl.Buffered(1))
    return pl.pallas_call(
        _ffn_kernel,
        grid=(bsz, seq // tm),
        in_specs=[
            tok(D_MODEL), tok(A_Q), tok(B_W), tok(C_W),
            pl.BlockSpec((None, None, 6, D_MODEL), lambda b, i: (layer, boff + b, 0, 0)),
            const2(p["g2"]), const2(p["w_out"]), const2(p["w_ff1"]), const2(p["w_ff2"]),
        ],
        out_specs=tok(D_MODEL),
        out_shape=jax.ShapeDtypeStruct((bsz, seq, D_MODEL), jnp.float32),
        compiler_params=_cparams(2),
        name="outproj_ffn",
    )(x, oa, ob, oc, mod, p["g2"], p["w_out"], p["w_ff1"], p["w_ff2"])


def _tile_heads(g, n_heads, scale=1.0):
    return (jnp.tile(g.astype(jnp.float32), n_heads) * scale).reshape(1, n_heads * HEAD_DIM)


def _rope_tables(seq):
    inv = ROPE_THETA ** (-jnp.arange(0, ROPE_DIMS, 2, dtype=jnp.float32) / ROPE_DIMS)
    ang = jnp.arange(seq, dtype=jnp.float32)[:, None] * inv[None, :]
    cos_sin = jnp.concatenate([jnp.cos(ang), jnp.sin(ang)], axis=1)
    half = ROPE_DIMS // 2
    select = np.zeros((ROPE_DIMS, 2 * LANES_V7X), np.float32)
    ones_row = np.zeros((1, 2 * LANES_V7X), np.float32)
    for lane in range(2 * LANES_V7X):
        is_sin, w = lane // LANES_V7X, lane % HEAD_DIM
        if w < ROPE_DIMS:
            select[is_sin * half + w % half, lane] = (-1.0 if w < half else 1.0) if is_sin else 1.0
        elif not is_sin:
            ones_row[0, lane] = 1.0
    return jnp.dot(cos_sin, jnp.asarray(select), precision=lax.Precision.HIGHEST) + jnp.asarray(ones_row)


def _band_mask():
    qi = np.arange(BLOCK)[:, None]
    kj = np.arange(3 * BLOCK)[None, :]
    ok = np.abs(kj - BLOCK - qi) <= BLOCK
    band = np.where(ok, 0.0, NEG_INF).astype(np.float32)
    return jnp.asarray(np.concatenate([band, band], axis=0))


def _nbr_bias_table(rel_bias):
    col = np.arange(GRID_W)
    cs = np.clip(col - NA_KW // 2, 0, GRID_W - NA_KW)
    col_in = (col[None, :] >= cs[:, None]) & (col[None, :] < cs[:, None] + NA_KW)
    dc = np.clip(col[None, :] - col[:, None] + NA_KW - 1, 0, 2 * NA_KW - 2)
    ndc = 2 * NA_KW - 1
    onehot = (dc.reshape(-1)[None, :] == np.arange(ndc)[:, None]).astype(np.float32)
    tm = jnp.einsum("hrd,dx->hrx", rel_bias.astype(jnp.float32), jnp.asarray(onehot),
                    precision=lax.Precision.HIGHEST)
    tm = jnp.where(jnp.asarray(col_in.reshape(-1))[None, None, :], tm * LOG2E, NEG_INF)
    tm = tm.reshape(C_HEADS, 2 * NA_KH - 1, GRID_W, GRID_W)
    pair = jnp.concatenate([tm[:, :-1], tm[:, 1:]], axis=-1)
    pair = pair.reshape(C_HEADS // 2, 2, 2 * NA_KH - 2, GRID_W, 2 * GRID_W)
    return pair.transpose(0, 2, 1, 3, 4).reshape(C_HEADS // 2, 2 * NA_KH - 2, 2 * GRID_W, 2 * GRID_W)


def _layer_params(l, seq_tables, norm1_g, norm2_g, w_in, qa_norm_g, ka_norm_g, a_sink, bv_norm_g,
                  w_spatial, b_spatial, qc_norm_g, kc_norm_g, c_rel_bias, w_out, w_ff1, w_ff2):
    order = np.asarray(A_HEAD_ORDER)
    q_cols = (order[:, None] * HEAD_DIM + np.arange(HEAD_DIM)[None, :]).reshape(-1)
    in_cols = np.concatenate([q_cols, np.arange(A_Q, D_IN)])
    out_rows = np.concatenate([q_cols, np.arange(A_Q, D_MIX)])
    seg = np.kron(np.eye(SEG_W // HEAD_DIM), np.ones((HEAD_DIM, HEAD_DIM))).astype(np.float32)
    ws = w_spatial[l].astype(jnp.bfloat16)
    return {
        "g1": norm1_g[l].reshape(1, D_MODEL),
        "g2": norm2_g[l].reshape(1, D_MODEL),
        "w_in": w_in[l][:, in_cols].astype(jnp.bfloat16),
        "w_out": w_out[l][out_rows, :].astype(jnp.bfloat16),
        "w_ff1": w_ff1[l].astype(jnp.bfloat16),
        "w_ff2": w_ff2[l].astype(jnp.bfloat16),
        "seg": jnp.asarray(seg, jnp.bfloat16),
        "ga": jnp.concatenate([_tile_heads(qa_norm_g[l], A_HEADS, ATTN_SCALE * LOG2E),
                               _tile_heads(ka_norm_g[l], A_KV_HEADS)], axis=1),
        "gc": jnp.concatenate([_tile_heads(qc_norm_g[l], C_HEADS, ATTN_SCALE * LOG2E),
                               _tile_heads(kc_norm_g[l], C_HEADS)], axis=1),
        "gbv": bv_norm_g[l].reshape(1, B_W),
        "rope": seq_tables,
        "ws": jnp.concatenate([ws[0::2], ws[1::2]], axis=2),
        "bs": jnp.repeat(b_spatial[l].T, HEAD_DIM, axis=1),
        "sink": a_sink[l][order].astype(jnp.float32) * LOG2E,
        "nbr_bias": _nbr_bias_table(c_rel_bias[l]),
    }


def _trunk(x, mod, boff, layer_params):
    for l, p in enumerate(layer_params):
        aq, ak, av, ob, cq, ck, cv = _inproj_call(x, mod, l, boff, p)
        oa = _window_call(aq, ak, av, p["sink"], p["band"])
        oc = _nbr_call(cq, ck, cv, p["nbr_bias"])
        x = _ffn_call(x, oa, ob, oc, mod, l, boff, p)
    return x


def kernel(x_prompt, x_sample, c_prompt, c_sample, norm1_g, norm2_g, w_ada, b_ada, w_in, qa_norm_g, ka_norm_g, a_sink, bv_norm_g, w_spatial, b_spatial, qc_norm_g, kc_norm_g, c_rel_bias, w_out, w_ff1, w_ff2):
    depth = w_in.shape[0]
    nb_prompt, seq_prompt, _ = x_prompt.shape
    nb_sample, seq_sample, _ = x_sample.shape
    assert nb_prompt + nb_sample <= 8
    for seq in (seq_prompt, seq_sample):
        assert seq % TM_IN == 0 and seq % TM_FFN == 0 and seq % TQ_WIN == 0 and seq // BLOCK >= 3

    c8 = jnp.zeros((8, D_MODEL), jnp.float32)
    c8 = c8.at[:nb_prompt].set(c_prompt).at[nb_prompt:nb_prompt + nb_sample].set(c_sample)
    mod = _ada_call(c8, w_ada, b_ada).reshape(depth, 8, 6, D_MODEL)

    seq_tables = _rope_tables(max(seq_prompt, seq_sample))
    band = _band_mask()
    layer_params = []
    for l in range(depth):
        p = _layer_params(l, seq_tables, norm1_g, norm2_g, w_in, qa_norm_g, ka_norm_g, a_sink,
                          bv_norm_g, w_spatial, b_spatial, qc_norm_g, kc_norm_g, c_rel_bias,
                          w_out, w_ff1, w_ff2)
        p["band"] = band
        layer_params.append(p)

    y_prompt = _trunk(x_prompt, mod, 0, layer_params)
    y_sample = _trunk(x_sample, mod, nb_prompt, layer_params)
    return (y_prompt, y_sample)
```

```python
from functools import partial

import numpy as np
import jax
import jax.numpy as jnp
from jax import lax
from jax.experimental import pallas as pl
from jax.experimental.pallas import tpu as pltpu

D_MODEL = 1024
HEAD_DIM = 64
A_HEADS = 6
A_KV_HEADS = 2
A_GROUP = A_HEADS // A_KV_HEADS
B_GROUPS = 4
C_HEADS = 6
A_Q = A_HEADS * HEAD_DIM
A_KV = A_KV_HEADS * HEAD_DIM
B_W = B_GROUPS * HEAD_DIM
C_W = C_HEADS * HEAD_DIM
D_IN = A_Q + 2 * A_KV + 2 * B_W + 3 * C_W
D_MIX = A_Q + B_W + C_W
BLOCK = 128
CHUNK = 128
GRID_W = 64
NA_KH = 8
NA_KW = 16
ROPE_THETA = 500000.0
ROPE_DIMS = HEAD_DIM // 4
D_FF = 4 * D_MODEL
EPS = 1e-6
ATTN_SCALE = HEAD_DIM ** -0.5
LOG2E = float(np.log2(np.e))

LANES_V7X = 128
MXU_DIM_V7X = 256
SEG_W = MXU_DIM_V7X
VMEM_LIMIT_BYTES_V7X = 56 * 1024 * 1024

TM_IN = 1024
TM_FFN = 1024
FF_CHUNK = 1024
TQ_WIN = 2048
TR_NBR = 32
KV_ROWS_BLK = 4
NBR_WIN_BLOCKS = TR_NBR // KV_ROWS_BLK + 2
ADA_TN = 1536
WIN_PIPE_LAGS = (2, 2)
NBR_PIPE_LAGS = (2, 3)

NEG_INF = float("-inf")

A_HEAD_ORDER = (0, 3, 1, 4, 2, 5)


def _cparams(n_axes):
    return pltpu.CompilerParams(
        dimension_semantics=("arbitrary",) * n_axes,
        vmem_limit_bytes=VMEM_LIMIT_BYTES_V7X,
    )


def _lane_is_low(shape):
    lane = lax.broadcasted_iota(jnp.int32, shape, len(shape) - 1)
    return (lane % LANES_V7X) < HEAD_DIM


def _ada_kernel(c_ref, w_ref, b_ref, o_ref):
    c = c_ref[...]
    s = (c * jax.nn.sigmoid(c)).astype(jnp.bfloat16)
    w = w_ref[...].astype(jnp.bfloat16)
    o_ref[...] = jnp.dot(s, w, preferred_element_type=jnp.float32) + b_ref[...]


def _ada_call(c8, w_ada, b_ada):
    depth = w_ada.shape[0]
    n = w_ada.shape[2]
    return pl.pallas_call(
        _ada_kernel,
        grid=(depth, n // ADA_TN),
        in_specs=[
            pl.BlockSpec((8, D_MODEL), lambda l, j: (0, 0)),
            pl.BlockSpec((None, D_MODEL, ADA_TN), lambda l, j: (l, 0, j)),
            pl.BlockSpec((None, 1, ADA_TN), lambda l, j: (l, 0, j)),
        ],
        out_specs=pl.BlockSpec((None, 8, ADA_TN), lambda l, j: (l, 0, j)),
        out_shape=jax.ShapeDtypeStruct((depth, 8, n), jnp.float32),
        compiler_params=_cparams(2),
        name="ada_mod",
    )(c8, w_ada, b_ada.reshape(depth, 1, n))


def _head_rms(x, seg_ones, gain):
    outs = []
    for j in range(x.shape[1] // SEG_W):
        xj = x[:, j * SEG_W:(j + 1) * SEG_W]
        ssum = jnp.dot((xj * xj).astype(jnp.bfloat16), seg_ones, preferred_element_type=jnp.float32)
        outs.append((xj * lax.rsqrt(ssum * (1.0 / HEAD_DIM) + EPS)) * gain[:, j * SEG_W:(j + 1) * SEG_W])
    return jnp.concatenate(outs, axis=1)


def _rope(x, cos_t, sin_t):
    n = x.shape[1] // LANES_V7X
    lane = lax.broadcasted_iota(jnp.int32, (1, LANES_V7X), 1) % HEAD_DIM
    first_half = lane < ROPE_DIMS // 2
    outs = []
    for j in range(n):
        xj = x[:, j * LANES_V7X:(j + 1) * LANES_V7X]
        up = pltpu.roll(xj, LANES_V7X - ROPE_DIMS // 2, 1)
        dn = pltpu.roll(xj, ROPE_DIMS // 2, 1)
        outs.append(xj * cos_t + jnp.where(first_half, up, dn) * sin_t)
    return outs[0] if n == 1 else jnp.concatenate(outs, axis=1)


def _inproj_kernel(x_ref, mod_ref, g1_ref, w_ref, seg_ref, ga_ref, gc_ref,
                   gbv_ref, rope_ref, ws_ref, bs_ref,
                   aq_ref, ak_ref, av_ref, ob_ref, cq_ref, ck_ref, cv_ref):
    x = x_ref[...]
    shift1 = mod_ref[0:1, :]
    scale1 = mod_ref[1:2, :]
    ms = jnp.mean(x * x, axis=-1, keepdims=True)
    h = ((x * lax.rsqrt(ms + EPS)) * g1_ref[...]) * (1.0 + scale1) + shift1
    proj = jnp.dot(h.astype(jnp.bfloat16), w_ref[...], preferred_element_type=jnp.float32)

    seg = seg_ref[...]
    o_av = A_Q + A_KV
    o_bu = o_av + A_KV
    o_bv = o_bu + B_W
    o_cq = o_bv + B_W
    o_cv = o_cq + 2 * C_W

    a_n = _rope(_head_rms(proj[:, 0:o_av], seg, ga_ref[...]),
                rope_ref[:, :LANES_V7X], rope_ref[:, LANES_V7X:])
    aq_ref[...] = a_n[:, :A_Q].astype(jnp.bfloat16)
    ak_ref[...] = a_n[:, A_Q:].astype(jnp.bfloat16)
    av_ref[...] = proj[:, o_av:o_bu].astype(jnp.bfloat16)

    c_n = _head_rms(proj[:, o_cq:o_cv], seg, gc_ref[...])
    cq_ref[...] = c_n[:, :C_W].astype(jnp.bfloat16)
    ck_ref[...] = c_n[:, C_W:].astype(jnp.bfloat16)
    cv_ref[...] = proj[:, o_cv:].astype(jnp.bfloat16)

    u = jax.nn.gelu(proj[:, o_bu:o_bv])
    v = jax.nn.gelu(proj[:, o_bv:o_cq])
    vms = jnp.mean(v * v, axis=-1, keepdims=True)
    vr = ((v * lax.rsqrt(vms + EPS)) * gbv_ref[...]).astype(jnp.bfloat16)
    low = _lane_is_low((CHUNK, LANES_V7X))
    zero = jnp.zeros((), jnp.bfloat16)
    tm = x.shape[0]
    for c in range(tm // CHUNK):
        rows = slice(c * CHUNK, (c + 1) * CHUNK)
        sv_parts = []
        for j in range(B_W // LANES_V7X):
            vj = vr[rows, j * LANES_V7X:(j + 1) * LANES_V7X]
            v2 = jnp.concatenate([jnp.where(low, vj, zero), jnp.where(low, zero, vj)], axis=0)
            sv_parts.append(jnp.dot(ws_ref[j], v2, preferred_element_type=jnp.float32))
        sv = jnp.concatenate(sv_parts, axis=1) + bs_ref[...]
        ob_ref[rows, :] = (u[rows, :] * sv).astype(jnp.bfloat16)


def _inproj_call(x, mod, layer, boff, p):
    bsz, seq, _ = x.shape
    tm = TM_IN
    tok = lambda w: pl.BlockSpec((None, tm, w), lambda b, i: (b, i, 0))
    const2 = lambda a: pl.BlockSpec(a.shape, lambda b, i: (0, 0))
    bf = lambda w: jax.ShapeDtypeStruct((bsz, seq, w), jnp.bfloat16)
    rope_spec = pl.BlockSpec((tm, 2 * LANES_V7X), lambda b, i: (i, 0))
    return pl.pallas_call(
        _inproj_kernel,
        grid=(bsz, seq // tm),
        in_specs=[
            tok(D_MODEL),
            pl.BlockSpec((None, None, 6, D_MODEL), lambda b, i: (layer, boff + b, 0, 0)),
            const2(p["g1"]), const2(p["w_in"]), const2(p["seg"]),
            const2(p["ga"]), const2(p["gc"]), const2(p["gbv"]),
            rope_spec,
            pl.BlockSpec(p["ws"].shape, lambda b, i: (0, 0, 0)),
            const2(p["bs"]),
        ],
        out_specs=[tok(A_Q), tok(A_KV), tok(A_KV), tok(B_W), tok(C_W), tok(C_W), tok(C_W)],
        out_shape=[bf(A_Q), bf(A_KV), bf(A_KV), bf(B_W), bf(C_W), bf(C_W), bf(C_W)],
        compiler_params=_cparams(2),
        name="inproj",
    )(x, mod, p["g1"], p["w_in"], p["seg"], p["ga"], p["gc"], p["gbv"],
      p["rope"], p["ws"], p["bs"])


_NT_DIMS = (((1,), (1,)), ((), ()))


def _qk_two_heads(q, k, low):
    zero = jnp.zeros((), q.dtype)
    q2 = jnp.concatenate([jnp.where(low, q, zero), jnp.where(low, zero, q)], axis=0)
    return lax.dot_general(q2, k, _NT_DIMS, preferred_element_type=jnp.float32)


def _softmax_numerator(s, extra_logit=None):
    m = jnp.max(s, axis=-1, keepdims=True)
    if extra_logit is None:
        return jnp.exp2(s - m).astype(jnp.bfloat16), None
    m = jnp.maximum(m, extra_logit)
    return jnp.exp2(s - m).astype(jnp.bfloat16), jnp.exp2(extra_logit - m)


def _pv_two_heads(e2, extra, v_ones, low):
    r = jnp.dot(e2, v_ones, preferred_element_type=jnp.float32)
    l2 = r[:, LANES_V7X:]
    if extra is not None:
        l2 = l2 + extra
    o2 = r[:, :LANES_V7X] / l2
    n = o2.shape[0] // 2
    return jnp.where(low, o2[:n], o2[n:]).astype(jnp.bfloat16)


def _software_pipeline(n, stages, lags):
    offs = [0]
    for lag in lags:
        offs.append(offs[-1] + lag)
    vals = [dict() for _ in stages]
    for step in range(n + offs[-1]):
        for si, fn in enumerate(stages):
            u = step - offs[si]
            if 0 <= u < n:
                vals[si][u] = fn(u) if si == 0 else fn(u, vals[si - 1].pop(u))


def _window_kernel(sink_ref, band_ref, q_ref, kp_ref, kc_ref, kn_ref, vp_ref, vc_ref, vn_ref, o_ref,
                   kwin, vwin):
    i = pl.program_id(1)
    last = pl.num_programs(1) - 1
    zero = jnp.zeros((), jnp.bfloat16)
    for lo_rows, ref in (((0, BLOCK), kp_ref), ((BLOCK, BLOCK + TQ_WIN), kc_ref),
                         ((BLOCK + TQ_WIN, 2 * BLOCK + TQ_WIN), kn_ref)):
        k_blk = ref[...]
        is_low = _lane_is_low(k_blk.shape)
        kwin[0, lo_rows[0]:lo_rows[1], :] = jnp.where(is_low, k_blk, zero)
        kwin[1, lo_rows[0]:lo_rows[1], :] = jnp.where(is_low, zero, k_blk)
    vwin[0:BLOCK, :A_KV] = vp_ref[...]
    vwin[BLOCK:BLOCK + TQ_WIN, :A_KV] = vc_ref[...]
    vwin[BLOCK + TQ_WIN:, :A_KV] = vn_ref[...]
    vwin[:, A_KV:] = jnp.ones((TQ_WIN + 2 * BLOCK, LANES_V7X), jnp.bfloat16)
    band_prev = band_ref[:, :BLOCK]
    band_next = band_ref[:, 2 * BLOCK:]
    is_low_head = lax.broadcasted_iota(jnp.int32, (2 * BLOCK, 1), 0) < BLOCK
    low = _lane_is_low((BLOCK, LANES_V7X))
    nt = TQ_WIN // BLOCK
    ng = A_Q // LANES_V7X
    band_prev_first = band_prev + jnp.where(i == 0, NEG_INF, 0.0)
    band_next_last = band_next + jnp.where(i == last, NEG_INF, 0.0)

    def unit(u):
        t, g = divmod(u, ng)
        return t, g, slice(t * BLOCK, (t + 1) * BLOCK), slice(g * LANES_V7X, (g + 1) * LANES_V7X)

    def qk(u):
        t, g, rows, cols = unit(u)
        keys = slice(t * BLOCK, (t + 3) * BLOCK)
        k2 = jnp.concatenate([kwin[0, keys, :], kwin[1, keys, :]], axis=0)
        s = lax.dot_general(q_ref[rows, cols], k2, _NT_DIMS, preferred_element_type=jnp.float32)
        return jnp.concatenate([s[:, :3 * BLOCK], s[:, 3 * BLOCK:]], axis=0)

    def sm(u, s):
        t, g, rows, cols = unit(u)
        m_prev = band_prev_first if t == 0 else band_prev
        m_next = band_next_last if t == nt - 1 else band_next
        s = jnp.concatenate([s[:, :BLOCK] + m_prev, s[:, BLOCK:2 * BLOCK], s[:, 2 * BLOCK:] + m_next], axis=1)
        sink = jnp.where(is_low_head, sink_ref[2 * g], sink_ref[2 * g + 1])
        return _softmax_numerator(s, sink)

    def pv(u, e):
        t, g, rows, cols = unit(u)
        o_ref[rows, cols] = _pv_two_heads(*e, vwin[t * BLOCK:(t + 3) * BLOCK, :], low)

    _software_pipeline(nt * ng, (qk, sm, pv), WIN_PIPE_LAGS)


def _window_call(aq, ak, av, sink, band):
    bsz, seq, _ = aq.shape
    nt = TQ_WIN // BLOCK
    nb = seq // BLOCK
    assert nt >= 2
    cur = lambda w: pl.BlockSpec((None, TQ_WIN, w), lambda b, i: (b, i, 0))
    prev = pl.BlockSpec((None, BLOCK, A_KV), lambda b, i: (b, jnp.maximum(i * nt - 1, 0), 0))
    nxt = pl.BlockSpec((None, BLOCK, A_KV), lambda b, i: (b, jnp.minimum(i * nt + nt, nb - 1), 0))
    kwin = pltpu.VMEM((2, TQ_WIN + 2 * BLOCK, A_KV), jnp.bfloat16)
    vwin = pltpu.VMEM((TQ_WIN + 2 * BLOCK, A_KV + LANES_V7X), jnp.bfloat16)
    return pl.pallas_call(
        _window_kernel,
        grid=(bsz, seq // TQ_WIN),
        in_specs=[
            pl.BlockSpec(memory_space=pltpu.SMEM),
            pl.BlockSpec(band.shape, lambda b, i: (0, 0)),
            cur(A_Q), prev, cur(A_KV), nxt, prev, cur(A_KV), nxt,
        ],
        out_specs=cur(A_Q),
        out_shape=jax.ShapeDtypeStruct((bsz, seq, A_Q), jnp.bfloat16),
        scratch_shapes=[kwin, vwin],
        compiler_params=_cparams(2),
        name="window_attn",
    )(sink, band, aq, ak, ak, ak, av, av, av)


def _nbr_kernel(rows, bias_ref, q_ref, *refs):
    nwin = NBR_WIN_BLOCKS
    k_blocks, v_blocks = refs[:nwin], refs[nwin:2 * nwin]
    o_ref, kwin, vwin = refs[2 * nwin:]
    i = pl.program_id(1)
    blk = KV_ROWS_BLK * GRID_W
    ng = C_W // LANES_V7X
    vw = 2 * LANES_V7X
    for n in range(nwin):
        kwin[n * blk:(n + 1) * blk, :] = k_blocks[n][...]
        for g in range(ng):
            vwin[n * blk:(n + 1) * blk, g * vw:g * vw + LANES_V7X] = (
                v_blocks[n][:, g * LANES_V7X:(g + 1) * LANES_V7X])
    for g in range(ng):
        vwin[:, g * vw + LANES_V7X:(g + 1) * vw] = jnp.ones((nwin * blk, LANES_V7X), jnp.bfloat16)
    r0 = i * TR_NBR
    win_start = _nbr_window_block(i, rows) * KV_ROWS_BLK
    low = _lane_is_low((GRID_W, LANES_V7X))
    nkeys = NA_KH * GRID_W
    key_off, bias_base = [], []
    for t in range(TR_NBR):
        r = r0 + t
        rs = jnp.clip(r - NA_KH // 2, 0, rows - NA_KH)
        key_off.append(pl.multiple_of((rs - win_start) * GRID_W, GRID_W))
        bias_base.append(NA_KH - 1 - (r - rs))

    def unit(u):
        t, g = divmod(u, ng)
        return t, g, slice(t * GRID_W, (t + 1) * GRID_W), slice(g * LANES_V7X, (g + 1) * LANES_V7X)

    def qk(u):
        t, g, qrows, cols = unit(u)
        return _qk_two_heads(q_ref[qrows, cols], kwin[pl.ds(key_off[t], nkeys), cols], low)

    def sm(u, s):
        t, g, qrows, cols = unit(u)
        bias = jnp.concatenate([bias_ref[g, bias_base[t] + 2 * m] for m in range(NA_KH // 2)], axis=1)
        return _softmax_numerator(s + bias)

    def pv(u, e):
        t, g, qrows, cols = unit(u)
        o_ref[qrows, cols] = _pv_two_heads(*e, vwin[pl.ds(key_off[t], nkeys), g * vw:(g + 1) * vw], low)

    _software_pipeline(TR_NBR * ng, (qk, sm, pv), NBR_PIPE_LAGS)


def _nbr_window_block(i, rows):
    per_tile = TR_NBR // KV_ROWS_BLK
    nblk = rows // KV_ROWS_BLK
    return jnp.clip(i * per_tile - 1, 0, nblk - (per_tile + 2))


def _nbr_call(cq, ck, cv, bias_tab):
    bsz, seq, _ = cq.shape
    rows = seq // GRID_W
    tq = TR_NBR * GRID_W
    blk = KV_ROWS_BLK * GRID_W
    nwin = NBR_WIN_BLOCKS
    assert rows % TR_NBR == 0 and rows >= nwin * KV_ROWS_BLK and NA_KH // 2 <= KV_ROWS_BLK
    kv_spec = lambda n: pl.BlockSpec((None, blk, C_W), lambda b, i: (b, _nbr_window_block(i, rows) + n, 0))
    q_spec = pl.BlockSpec((None, tq, C_W), lambda b, i: (b, i, 0))
    kv_specs = [kv_spec(n) for n in range(nwin)]
    return pl.pallas_call(
        partial(_nbr_kernel, rows),
        grid=(bsz, rows // TR_NBR),
        in_specs=[pl.BlockSpec(bias_tab.shape, lambda b, i: (0, 0, 0, 0)), q_spec] + kv_specs + kv_specs,
        out_specs=q_spec,
        out_shape=jax.ShapeDtypeStruct((bsz, seq, C_W), jnp.bfloat16),
        scratch_shapes=[pltpu.VMEM((nwin * blk, C_W), jnp.bfloat16),
                        pltpu.VMEM((nwin * blk, 2 * C_W), jnp.bfloat16)],
        compiler_params=_cparams(2),
        name="nbr_attn",
    )(bias_tab, cq, *([ck] * nwin), *([cv] * nwin))


def _ffn_kernel(x_ref, oa_ref, ob_ref, oc_ref, mod_ref, g2_ref, wo_ref, w1_ref, w2_ref, y_ref):
    gate1 = mod_ref[2:3, :]
    shift2 = mod_ref[3:4, :]
    scale2 = mod_ref[4:5, :]
    gate2 = mod_ref[5:6, :]
    mix_in = jnp.concatenate([oa_ref[...], ob_ref[...], oc_ref[...]], axis=1)
    mix = jnp.dot(mix_in, wo_ref[...], preferred_element_type=jnp.float32)
    x1 = x_ref[...] + gate1 * mix
    ms = jnp.mean(x1 * x1, axis=-1, keepdims=True)
    h2 = (((x1 * lax.rsqrt(ms + EPS)) * g2_ref[...]) * (1.0 + scale2) + shift2).astype(jnp.bfloat16)
    ff = None
    for j in range(D_FF // FF_CHUNK):
        cols = slice(j * FF_CHUNK, (j + 1) * FF_CHUNK)
        a = jnp.maximum(jnp.dot(h2, w1_ref[:, cols], preferred_element_type=jnp.float32), 0.0)
        part = jnp.dot((a * a).astype(jnp.bfloat16), w2_ref[cols, :], preferred_element_type=jnp.float32)
        ff = part if ff is None else ff + part
    y_ref[...] = x1 + gate2 * ff


def _ffn_call(x, oa, ob, oc, mod, layer, boff, p):
    bsz, seq, _ = x.shape
    tm = TM_FFN
    tok = lambda w: pl.BlockSpec((None, tm, w), lambda b, i: (b, i, 0))
    const2 = lambda a: pl.BlockSpec(a.shape, lambda b, i: (0, 0), pipeline_mode=pl.Buffered(1))
    return pl.pallas_call(
        _ffn_kernel,
        grid=(bsz, seq // tm),
        in_specs=[
            tok(D_MODEL), tok(A_Q), tok(B_W), tok(C_W),
            pl.BlockSpec((None, None, 6, D_MODEL), lambda b, i: (layer, boff + b, 0, 0)),
            const2(p["g2"]), const2(p["w_out"]), const2(p["w_ff1"]), const2(p["w_ff2"]),
        ],
        out_specs=tok(D_MODEL),
        out_shape=jax.ShapeDtypeStruct((bsz, seq, D_MODEL), jnp.float32),
        compiler_params=_cparams(2),
        name="outproj_ffn",
    )(x, oa, ob, oc, mod, p["g2"], p["w_out"], p["w_ff1"], p["w_ff2"])


def _tile_heads(g, n_heads, scale=1.0):
    return (jnp.tile(g.astype(jnp.float32), n_heads) * scale).reshape(1, n_heads * HEAD_DIM)


def _rope_tables(seq):
    inv = ROPE_THETA ** (-jnp.arange(0, ROPE_DIMS, 2, dtype=jnp.float32) / ROPE_DIMS)
    ang = jnp.arange(seq, dtype=jnp.float32)[:, None] * inv[None, :]
    cos_sin = jnp.concatenate([jnp.cos(ang), jnp.sin(ang)], axis=1)
    half = ROPE_DIMS // 2
    select = np.zeros((ROPE_DIMS, 2 * LANES_V7X), np.float32)
    ones_row = np.zeros((1, 2 * LANES_V7X), np.float32)
    for lane in range(2 * LANES_V7X):
        is_sin, w = lane // LANES_V7X, lane % HEAD_DIM
        if w < ROPE_DIMS:
            select[is_sin * half + w % half, lane] = (-1.0 if w < half else 1.0) if is_sin else 1.0
        elif not is_sin:
            ones_row[0, lane] = 1.0
    return jnp.dot(cos_sin, jnp.asarray(select), precision=lax.Precision.HIGHEST) + jnp.asarray(ones_row)


def _band_mask():
    qi = np.arange(BLOCK)[:, None]
    kj = np.arange(3 * BLOCK)[None, :]
    ok = np.abs(kj - BLOCK - qi) <= BLOCK
    band = np.where(ok, 0.0, NEG_INF).astype(np.float32)
    return jnp.asarray(np.concatenate([band, band], axis=0))


def _nbr_bias_table(rel_bias):
    col = np.arange(GRID_W)
    cs = np.clip(col - NA_KW // 2, 0, GRID_W - NA_KW)
    col_in = (col[None, :] >= cs[:, None]) & (col[None, :] < cs[:, None] + NA_KW)
    dc = np.clip(col[None, :] - col[:, None] + NA_KW - 1, 0, 2 * NA_KW - 2)
    ndc = 2 * NA_KW - 1
    onehot = (dc.reshape(-1)[None, :] == np.arange(ndc)[:, None]).astype(np.float32)
    tm = jnp.einsum("hrd,dx->hrx", rel_bias.astype(jnp.float32), jnp.asarray(onehot),
                    precision=lax.Precision.HIGHEST)
    tm = jnp.where(jnp.asarray(col_in.reshape(-1))[None, None, :], tm * LOG2E, NEG_INF)
    tm = tm.reshape(C_HEADS, 2 * NA_KH - 1, GRID_W, GRID_W)
    pair = jnp.concatenate([tm[:, :-1], tm[:, 1:]], axis=-1)
    pair = pair.reshape(C_HEADS // 2, 2, 2 * NA_KH - 2, GRID_W, 2 * GRID_W)
    return pair.transpose(0, 2, 1, 3, 4).reshape(C_HEADS // 2, 2 * NA_KH - 2, 2 * GRID_W, 2 * GRID_W)


def _layer_params(l, seq_tables, norm1_g, norm2_g, w_in, qa_norm_g, ka_norm_g, a_sink, bv_norm_g,
                  w_spatial, b_spatial, qc_norm_g, kc_norm_g, c_rel_bias, w_out, w_ff1, w_ff2):
    order = np.asarray(A_HEAD_ORDER)
    q_cols = (order[:, None] * HEAD_DIM + np.arange(HEAD_DIM)[None, :]).reshape(-1)
    in_cols = np.concatenate([q_cols, np.arange(A_Q, D_IN)])
    out_rows = np.concatenate([q_cols, np.arange(A_Q, D_MIX)])
    seg = np.kron(np.eye(SEG_W // HEAD_DIM), np.ones((HEAD_DIM, HEAD_DIM))).astype(np.float32)
    ws = w_spatial[l].astype(jnp.bfloat16)
    return {
        "g1": norm1_g[l].reshape(1, D_MODEL),
        "g2": norm2_g[l].reshape(1, D_MODEL),
        "w_in": w_in[l][:, in_cols].astype(jnp.bfloat16),
        "w_out": w_out[l][out_rows, :].astype(jnp.bfloat16),
        "w_ff1": w_ff1[l].astype(jnp.bfloat16),
        "w_ff2": w_ff2[l].astype(jnp.bfloat16),
        "seg": jnp.asarray(seg, jnp.bfloat16),
        "ga": jnp.concatenate([_tile_heads(qa_norm_g[l], A_HEADS, ATTN_SCALE * LOG2E),
                               _tile_heads(ka_norm_g[l], A_KV_HEADS)], axis=1),
        "gc": jnp.concatenate([_tile_heads(qc_norm_g[l], C_HEADS, ATTN_SCALE * LOG2E),
                               _tile_heads(kc_norm_g[l], C_HEADS)], axis=1),
        "gbv": bv_norm_g[l].reshape(1, B_W),
        "rope": seq_tables,
        "ws": jnp.concatenate([ws[0::2], ws[1::2]], axis=2),
        "bs": jnp.repeat(b_spatial[l].T, HEAD_DIM, axis=1),
        "sink": a_sink[l][order].astype(jnp.float32) * LOG2E,
        "nbr_bias": _nbr_bias_table(c_rel_bias[l]),
    }


def _trunk(x, mod, boff, layer_params):
    for l, p in enumerate(layer_params):
        aq, ak, av, ob, cq, ck, cv = _inproj_call(x, mod, l, boff, p)
        oa = _window_call(aq, ak, av, p["sink"], p["band"])
        oc = _nbr_call(cq, ck, cv, p["nbr_bias"])
        x = _ffn_call(x, oa, ob, oc, mod, l, boff, p)
    return x


def kernel(x_prompt, x_sample, c_prompt, c_sample, norm1_g, norm2_g, w_ada, b_ada, w_in, qa_norm_g, ka_norm_g, a_sink, bv_norm_g, w_spatial, b_spatial, qc_norm_g, kc_norm_g, c_rel_bias, w_out, w_ff1, w_ff2):
    depth = w_in.shape[0]
    nb_prompt, seq_prompt, _ = x_prompt.shape
    nb_sample, seq_sample, _ = x_sample.shape
    assert nb_prompt + nb_sample <= 8
    for seq in (seq_prompt, seq_sample):
        assert seq % TM_IN == 0 and seq % TM_FFN == 0 and seq % TQ_WIN == 0 and seq // BLOCK >= 3

    c8 = jnp.zeros((8, D_MODEL), jnp.float32)
    c8 = c8.at[:nb_prompt].set(c_prompt).at[nb_prompt:nb_prompt + nb_sample].set(c_sample)
    mod = _ada_call(c8, w_ada, b_ada).reshape(depth, 8, 6, D_MODEL)

    seq_tables = _rope_tables(max(seq_prompt, seq_sample))
    band = _band_mask()
    layer_params = []
    for l in range(depth):
        p = _layer_params(l, seq_tables, norm1_g, norm2_g, w_in, qa_norm_g, ka_norm_g, a_sink,
                          bv_norm_g, w_spatial, b_spatial, qc_norm_g, kc_norm_g, c_rel_bias,
                          w_out, w_ff1, w_ff2)
        p["band"] = band
        layer_params.append(p)

    y_prompt = _trunk(x_prompt, mod, 0, layer_params)
    y_sample = _trunk(x_sample, mod, nb_prompt, layer_params)
    return (y_prompt, y_sample)
```

```python
from functools import partial

import numpy as np
import jax
import jax.numpy as jnp
from jax import lax
from jax.experimental import pallas as pl
from jax.experimental.pallas import tpu as pltpu

D_MODEL = 1024
HEAD_DIM = 64
A_HEADS = 6
A_KV_HEADS = 2
A_GROUP = A_HEADS // A_KV_HEADS
B_GROUPS = 4
C_HEADS = 6
A_Q = A_HEADS * HEAD_DIM
A_KV = A_KV_HEADS * HEAD_DIM
B_W = B_GROUPS * HEAD_DIM
C_W = C_HEADS * HEAD_DIM
D_IN = A_Q + 2 * A_KV + 2 * B_W + 3 * C_W
D_MIX = A_Q + B_W + C_W
BLOCK = 128
CHUNK = 128
GRID_W = 64
NA_KH = 8
NA_KW = 16
ROPE_THETA = 500000.0
ROPE_DIMS = HEAD_DIM // 4
D_FF = 4 * D_MODEL
EPS = 1e-6
ATTN_SCALE = HEAD_DIM ** -0.5
LOG2E = float(np.log2(np.e))

LANES_V7X = 128
MXU_DIM_V7X = 256
SEG_W = MXU_DIM_V7X
VMEM_LIMIT_BYTES_V7X = 56 * 1024 * 1024

TM_IN = 1024
TM_FFN = 1024
FF_CHUNK = 1024
TQ_WIN = 2048
TR_NBR = 32
KV_ROWS_BLK = 4
NBR_WIN_BLOCKS = TR_NBR // KV_ROWS_BLK + 2
ADA_TN = 1536
WIN_PIPE_LAGS = (2, 2)
NBR_PIPE_LAGS = (2, 3)

NEG_INF = float("-inf")

A_HEAD_ORDER = (0, 3, 1, 4, 2, 5)


def _cparams(n_axes):
    return pltpu.CompilerParams(
        dimension_semantics=("arbitrary",) * n_axes,
        vmem_limit_bytes=VMEM_LIMIT_BYTES_V7X,
    )


def _lane_is_low(shape):
    lane = lax.broadcasted_iota(jnp.int32, shape, len(shape) - 1)
    return (lane % LANES_V7X) < HEAD_DIM


def _ada_kernel(c_ref, w_ref, b_ref, o_ref):
    c = c_ref[...]
    s = (c * jax.nn.sigmoid(c)).astype(jnp.bfloat16)
    w = w_ref[...].astype(jnp.bfloat16)
    o_ref[...] = jnp.dot(s, w, preferred_element_type=jnp.float32) + b_ref[...]


def _ada_call(c8, w_ada, b_ada):
    depth = w_ada.shape[0]
    n = w_ada.shape[2]
    return pl.pallas_call(
        _ada_kernel,
        grid=(depth, n // ADA_TN),
        in_specs=[
            pl.BlockSpec((8, D_MODEL), lambda l, j: (0, 0)),
            pl.BlockSpec((None, D_MODEL, ADA_TN), lambda l, j: (l, 0, j)),
            pl.BlockSpec((None, 1, ADA_TN), lambda l, j: (l, 0, j)),
        ],
        out_specs=pl.BlockSpec((None, 8, ADA_TN), lambda l, j: (l, 0, j)),
        out_shape=jax.ShapeDtypeStruct((depth, 8, n), jnp.float32),
        compiler_params=_cparams(2),
        name="ada_mod",
    )(c8, w_ada, b_ada.reshape(depth, 1, n))


def _head_rms(x, seg_ones, gain):
    outs = []
    for j in range(x.shape[1] // SEG_W):
        xj = x[:, j * SEG_W:(j + 1) * SEG_W]
        ssum = jnp.dot((xj * xj).astype(jnp.bfloat16), seg_ones, preferred_element_type=jnp.float32)
        outs.append((xj * lax.rsqrt(ssum * (1.0 / HEAD_DIM) + EPS)) * gain[:, j * SEG_W:(j + 1) * SEG_W])
    return jnp.concatenate(outs, axis=1)


def _rope(x, cos_t, sin_t):
    n = x.shape[1] // LANES_V7X
    lane = lax.broadcasted_iota(jnp.int32, (1, LANES_V7X), 1) % HEAD_DIM
    first_half = lane < ROPE_DIMS // 2
    outs = []
    for j in range(n):
        xj = x[:, j * LANES_V7X:(j + 1) * LANES_V7X]
        up = pltpu.roll(xj, LANES_V7X - ROPE_DIMS // 2, 1)
        dn = pltpu.roll(xj, ROPE_DIMS // 2, 1)
        outs.append(xj * cos_t + jnp.where(first_half, up, dn) * sin_t)
    return outs[0] if n == 1 else jnp.concatenate(outs, axis=1)


def _inproj_kernel(x_ref, mod_ref, g1_ref, w_ref, seg_ref, ga_ref, gc_ref,
                   gbv_ref, rope_ref, ws_ref, bs_ref,
                   aq_ref, ak_ref, av_ref, ob_ref, cq_ref, ck_ref, cv_ref):
    x = x_ref[...]
    shift1 = mod_ref[0:1, :]
    scale1 = mod_ref[1:2, :]
    ms = jnp.mean(x * x, axis=-1, keepdims=True)
    h = ((x * lax.rsqrt(ms + EPS)) * g1_ref[...]) * (1.0 + scale1) + shift1
    proj = jnp.dot(h.astype(jnp.bfloat16), w_ref[...], preferred_element_type=jnp.float32)

    seg = seg_ref[...]
    o_av = A_Q + A_KV
    o_bu = o_av + A_KV
    o_bv = o_bu + B_W
    o_cq = o_bv + B_W
    o_cv = o_cq + 2 * C_W

    a_n = _rope(_head_rms(proj[:, 0:o_av], seg, ga_ref[...]),
                rope_ref[:, :LANES_V7X], rope_ref[:, LANES_V7X:])
    aq_ref[...] = a_n[:, :A_Q].astype(jnp.bfloat16)
    ak_ref[...] = a_n[:, A_Q:].astype(jnp.bfloat16)
    av_ref[...] = proj[:, o_av:o_bu].astype(jnp.bfloat16)

    c_n = _head_rms(proj[:, o_cq:o_cv], seg, gc_ref[...])
    cq_ref[...] = c_n[:, :C_W].astype(jnp.bfloat16)
    ck_ref[...] = c_n[:, C_W:].astype(jnp.bfloat16)
    cv_ref[...] = proj[:, o_cv:].astype(jnp.bfloat16)

    u = jax.nn.gelu(proj[:, o_bu:o_bv])
    v = jax.nn.gelu(proj[:, o_bv:o_cq])
    vms = jnp.mean(v * v, axis=-1, keepdims=True)
    vr = ((v * lax.rsqrt(vms + EPS)) * gbv_ref[...]).astype(jnp.bfloat16)
    low = _lane_is_low((CHUNK, LANES_V7X))
    zero = jnp.zeros((), jnp.bfloat16)
    tm = x.shape[0]
    for c in range(tm // CHUNK):
        rows = slice(c * CHUNK, (c + 1) * CHUNK)
        sv_parts = []
        for j in range(B_W // LANES_V7X):
            vj = vr[rows, j * LANES_V7X:(j + 1) * LANES_V7X]
            v2 = jnp.concatenate([jnp.where(low, vj, zero), jnp.where(low, zero, vj)], axis=0)
            sv_parts.append(jnp.dot(ws_ref[j], v2, preferred_element_type=jnp.float32))
        sv = jnp.concatenate(sv_parts, axis=1) + bs_ref[...]
        ob_ref[rows, :] = (u[rows, :] * sv).astype(jnp.bfloat16)


def _inproj_call(x, mod, layer, boff, p):
    bsz, seq, _ = x.shape
    tm = TM_IN
    tok = lambda w: pl.BlockSpec((None, tm, w), lambda b, i: (b, i, 0))
    const2 = lambda a: pl.BlockSpec(a.shape, lambda b, i: (0, 0))
    bf = lambda w: jax.ShapeDtypeStruct((bsz, seq, w), jnp.bfloat16)
    rope_spec = pl.BlockSpec((tm, 2 * LANES_V7X), lambda b, i: (i, 0))
    return pl.pallas_call(
        _inproj_kernel,
        grid=(bsz, seq // tm),
        in_specs=[
            tok(D_MODEL),
            pl.BlockSpec((None, None, 6, D_MODEL), lambda b, i: (layer, boff + b, 0, 0)),
            const2(p["g1"]), const2(p["w_in"]), const2(p["seg"]),
            const2(p["ga"]), const2(p["gc"]), const2(p["gbv"]),
            rope_spec,
            pl.BlockSpec(p["ws"].shape, lambda b, i: (0, 0, 0)),
            const2(p["bs"]),
        ],
        out_specs=[tok(A_Q), tok(A_KV), tok(A_KV), tok(B_W), tok(C_W), tok(C_W), tok(C_W)],
        out_shape=[bf(A_Q), bf(A_KV), bf(A_KV), bf(B_W), bf(C_W), bf(C_W), bf(C_W)],
        compiler_params=_cparams(2),
        name="inproj",
    )(x, mod, p["g1"], p["w_in"], p["seg"], p["ga"], p["gc"], p["gbv"],
      p["rope"], p["ws"], p["bs"])


_NT_DIMS = (((1,), (1,)), ((), ()))


def _qk_two_heads(q, k, low):
    zero = jnp.zeros((), q.dtype)
    q2 = jnp.concatenate([jnp.where(low, q, zero), jnp.where(low, zero, q)], axis=0)
    return lax.dot_general(q2, k, _NT_DIMS, preferred_element_type=jnp.float32)


def _softmax_numerator(s, extra_logit=None):
    m = jnp.max(s, axis=-1, keepdims=True)
    if extra_logit is None:
        return jnp.exp2(s - m).astype(jnp.bfloat16), None
    m = jnp.maximum(m, extra_logit)
    return jnp.exp2(s - m).astype(jnp.bfloat16), jnp.exp2(extra_logit - m)


def _pv_two_heads(e2, extra, v_ones, low):
    r = jnp.dot(e2, v_ones, preferred_element_type=jnp.float32)
    l2 = r[:, LANES_V7X:]
    if extra is not None:
        l2 = l2 + extra
    o2 = r[:, :LANES_V7X] / l2
    n = o2.shape[0] // 2
    return jnp.where(low, o2[:n], o2[n:]).astype(jnp.bfloat16)


def _software_pipeline(n, stages, lags):
    offs = [0]
    for lag in lags:
        offs.append(offs[-1] + lag)
    vals = [dict() for _ in stages]
    for step in range(n + offs[-1]):
        for si, fn in enumerate(stages):
            u = step - offs[si]
            if 0 <= u < n:
                vals[si][u] = fn(u) if si == 0 else fn(u, vals[si - 1].pop(u))


def _window_kernel(sink_ref, band_ref, q_ref, kp_ref, kc_ref, kn_ref, vp_ref, vc_ref, vn_ref, o_ref,
                   kwin, vwin):
    i = pl.program_id(1)
    last = pl.num_programs(1) - 1
    kwin[0:BLOCK, :] = kp_ref[...]
    kwin[BLOCK:BLOCK + TQ_WIN, :] = kc_ref[...]
    kwin[BLOCK + TQ_WIN:, :] = kn_ref[...]
    vwin[0:BLOCK, :A_KV] = vp_ref[...]
    vwin[BLOCK:BLOCK + TQ_WIN, :A_KV] = vc_ref[...]
    vwin[BLOCK + TQ_WIN:, :A_KV] = vn_ref[...]
    vwin[:, A_KV:] = jnp.ones((TQ_WIN + 2 * BLOCK, LANES_V7X), jnp.bfloat16)
    band_prev = band_ref[:, :BLOCK]
    band_next = band_ref[:, 2 * BLOCK:]
    is_low_head = lax.broadcasted_iota(jnp.int32, (2 * BLOCK, 1), 0) < BLOCK
    low = _lane_is_low((BLOCK, LANES_V7X))
    nt = TQ_WIN // BLOCK
    ng = A_Q // LANES_V7X
    band_prev_first = band_prev + jnp.where(i == 0, NEG_INF, 0.0)
    band_next_last = band_next + jnp.where(i == last, NEG_INF, 0.0)

    def unit(u):
        t, g = divmod(u, ng)
        return t, g, slice(t * BLOCK, (t + 1) * BLOCK), slice(g * LANES_V7X, (g + 1) * LANES_V7X)

    def qk(u):
        t, g, rows, cols = unit(u)
        return _qk_two_heads(q_ref[rows, cols], kwin[t * BLOCK:(t + 3) * BLOCK, :], low)

    def sm(u, s):
        t, g, rows, cols = unit(u)
        m_prev = band_prev_first if t == 0 else band_prev
        m_next = band_next_last if t == nt - 1 else band_next
        s = jnp.concatenate([s[:, :BLOCK] + m_prev, s[:, BLOCK:2 * BLOCK], s[:, 2 * BLOCK:] + m_next], axis=1)
        sink = jnp.where(is_low_head, sink_ref[2 * g], sink_ref[2 * g + 1])
        return _softmax_numerator(s, sink)

    def pv(u, e):
        t, g, rows, cols = unit(u)
        o_ref[rows, cols] = _pv_two_heads(*e, vwin[t * BLOCK:(t + 3) * BLOCK, :], low)

    _software_pipeline(nt * ng, (qk, sm, pv), WIN_PIPE_LAGS)


def _window_call(aq, ak, av, sink, band):
    bsz, seq, _ = aq.shape
    nt = TQ_WIN // BLOCK
    nb = seq // BLOCK
    assert nt >= 2
    cur = lambda w: pl.BlockSpec((None, TQ_WIN, w), lambda b, i: (b, i, 0))
    prev = pl.BlockSpec((None, BLOCK, A_KV), lambda b, i: (b, jnp.maximum(i * nt - 1, 0), 0))
    nxt = pl.BlockSpec((None, BLOCK, A_KV), lambda b, i: (b, jnp.minimum(i * nt + nt, nb - 1), 0))
    kwin = pltpu.VMEM((TQ_WIN + 2 * BLOCK, A_KV), jnp.bfloat16)
    vwin = pltpu.VMEM((TQ_WIN + 2 * BLOCK, A_KV + LANES_V7X), jnp.bfloat16)
    return pl.pallas_call(
        _window_kernel,
        grid=(bsz, seq // TQ_WIN),
        in_specs=[
            pl.BlockSpec(memory_space=pltpu.SMEM),
            pl.BlockSpec(band.shape, lambda b, i: (0, 0)),
            cur(A_Q), prev, cur(A_KV), nxt, prev, cur(A_KV), nxt,
        ],
        out_specs=cur(A_Q),
        out_shape=jax.ShapeDtypeStruct((bsz, seq, A_Q), jnp.bfloat16),
        scratch_shapes=[kwin, vwin],
        compiler_params=_cparams(2),
        name="window_attn",
    )(sink, band, aq, ak, ak, ak, av, av, av)


def _nbr_kernel(rows, bias_ref, q_ref, *refs):
    nwin = NBR_WIN_BLOCKS
    k_blocks, v_blocks = refs[:nwin], refs[nwin:2 * nwin]
    o_ref, kwin, vwin = refs[2 * nwin:]
    i = pl.program_id(1)
    blk = KV_ROWS_BLK * GRID_W
    ng = C_W // LANES_V7X
    vw = 2 * LANES_V7X
    for n in range(nwin):
        kwin[n * blk:(n + 1) * blk, :] = k_blocks[n][...]
        for g in range(ng):
            vwin[n * blk:(n + 1) * blk, g * vw:g * vw + LANES_V7X] = (
                v_blocks[n][:, g * LANES_V7X:(g + 1) * LANES_V7X])
    for g in range(ng):
        vwin[:, g * vw + LANES_V7X:(g + 1) * vw] = jnp.ones((nwin * blk, LANES_V7X), jnp.bfloat16)
    r0 = i * TR_NBR
    win_start = _nbr_window_block(i, rows) * KV_ROWS_BLK
    low = _lane_is_low((GRID_W, LANES_V7X))
    nkeys = NA_KH * GRID_W
    key_off, bias_base = [], []
    for t in range(TR_NBR):
        r = r0 + t
        rs = jnp.clip(r - NA_KH // 2, 0, rows - NA_KH)
        key_off.append(pl.multiple_of((rs - win_start) * GRID_W, GRID_W))
        bias_base.append(NA_KH - 1 - (r - rs))

    def unit(u):
        t, g = divmod(u, ng)
        return t, g, slice(t * GRID_W, (t + 1) * GRID_W), slice(g * LANES_V7X, (g + 1) * LANES_V7X)

    def qk(u):
        t, g, qrows, cols = unit(u)
        return _qk_two_heads(q_ref[qrows, cols], kwin[pl.ds(key_off[t], nkeys), cols], low)

    def sm(u, s):
        t, g, qrows, cols = unit(u)
        bias = jnp.concatenate([bias_ref[g, bias_base[t] + 2 * m] for m in range(NA_KH // 2)], axis=1)
        return _softmax_numerator(s + bias)

    def pv(u, e):
        t, g, qrows, cols = unit(u)
        o_ref[qrows, cols] = _pv_two_heads(*e, vwin[pl.ds(key_off[t], nkeys), g * vw:(g + 1) * vw], low)

    _software_pipeline(TR_NBR * ng, (qk, sm, pv), NBR_PIPE_LAGS)


def _nbr_window_block(i, rows):
    per_tile = TR_NBR // KV_ROWS_BLK
    nblk = rows // KV_ROWS_BLK
    return jnp.clip(i * per_tile - 1, 0, nblk - (per_tile + 2))


def _nbr_call(cq, ck, cv, bias_tab):
    bsz, seq, _ = cq.shape
    rows = seq // GRID_W
    tq = TR_NBR * GRID_W
    blk = KV_ROWS_BLK * GRID_W
    nwin = NBR_WIN_BLOCKS
    assert rows % TR_NBR == 0 and rows >= nwin * KV_ROWS_BLK and NA_KH // 2 <= KV_ROWS_BLK
    kv_spec = lambda n: pl.BlockSpec((None, blk, C_W), lambda b, i: (b, _nbr_window_block(i, rows) + n, 0))
    q_spec = pl.BlockSpec((None, tq, C_W), lambda b, i: (b, i, 0))
    kv_specs = [kv_spec(n) for n in range(nwin)]
    return pl.pallas_call(
        partial(_nbr_kernel, rows),
        grid=(bsz, rows // TR_NBR),
        in_specs=[pl.BlockSpec(bias_tab.shape, lambda b, i: (0, 0, 0, 0)), q_spec] + kv_specs + kv_specs,
        out_specs=q_spec,
        out_shape=jax.ShapeDtypeStruct((bsz, seq, C_W), jnp.bfloat16),
        scratch_shapes=[pltpu.VMEM((nwin * blk, C_W), jnp.bfloat16),
                        pltpu.VMEM((nwin * blk, 2 * C_W), jnp.bfloat16)],
        compiler_params=_cparams(2),
        name="nbr_attn",
    )(bias_tab, cq, *([ck] * nwin), *([cv] * nwin))


def _ffn_kernel(x_ref, oa_ref, ob_ref, oc_ref, mod_ref, g2_ref, wo_ref, w1_ref, w2_ref, y_ref):
    gate1 = mod_ref[2:3, :]
    shift2 = mod_ref[3:4, :]
    scale2 = mod_ref[4:5, :]
    gate2 = mod_ref[5:6, :]
    mix_in = jnp.concatenate([oa_ref[...], ob_ref[...], oc_ref[...]], axis=1)
    mix = jnp.dot(mix_in, wo_ref[...], preferred_element_type=jnp.float32)
    x1 = x_ref[...] + gate1 * mix
    ms = jnp.mean(x1 * x1, axis=-1, keepdims=True)
    h2 = (((x1 * lax.rsqrt(ms + EPS)) * g2_ref[...]) * (1.0 + scale2) + shift2).astype(jnp.bfloat16)
    ff = None
    for j in range(D_FF // FF_CHUNK):
        cols = slice(j * FF_CHUNK, (j + 1) * FF_CHUNK)
        a = jnp.maximum(jnp.dot(h2, w1_ref[:, cols], preferred_element_type=jnp.float32), 0.0)
        part = jnp.dot((a * a).astype(jnp.bfloat16), w2_ref[cols, :], preferred_element_type=jnp.float32)
        ff = part if ff is None else ff + part
    y_ref[...] = x1 + gate2 * ff


def _ffn_call(x, oa, ob, oc, mod, layer, boff, p):
    bsz, seq, _ = x.shape
    tm = TM_FFN
    tok = lambda w: pl.BlockSpec((None, tm, w), lambda b, i: (b, i, 0))
    const2 = lambda a: pl.BlockSpec(a.shape, lambda b, i: (0, 0), pipeline_mode=pl.Buffered(1))
    return pl.pallas_call(
        _ffn_kernel,
        grid=(bsz, seq // tm),
        in_specs=[
            tok(D_MODEL), tok(A_Q), tok(B_W), tok(C_W),
            pl.BlockSpec((None, None, 6, D_MODEL), lambda b, i: (layer, boff + b, 0, 0)),
            const2(p["g2"]), const2(p["w_out"]), const2(p["w_ff1"]), const2(p["w_ff2"]),
        ],
        out_specs=tok(D_MODEL),
        out_shape=jax.ShapeDtypeStruct((bsz, seq, D_MODEL), jnp.float32),
        compiler_params=_cparams(2),
        name="outproj_ffn",
    )(x, oa, ob, oc, mod, p["g2"], p["w_out"], p["w_ff1"], p["w_ff2"])


def _tile_heads(g, n_heads, scale=1.0):
    return (jnp.tile(g.astype(jnp.float32), n_heads) * scale).reshape(1, n_heads * HEAD_DIM)


def _rope_tables(seq):
    inv = ROPE_THETA ** (-jnp.arange(0, ROPE_DIMS, 2, dtype=jnp.float32) / ROPE_DIMS)
    ang = jnp.arange(seq, dtype=jnp.float32)[:, None] * inv[None, :]
    cos_sin = jnp.concatenate([jnp.cos(ang), jnp.sin(ang)], axis=1)
    half = ROPE_DIMS // 2
    select = np.zeros((ROPE_DIMS, 2 * LANES_V7X), np.float32)
    ones_row = np.zeros((1, 2 * LANES_V7X), np.float32)
    for lane in range(2 * LANES_V7X):
        is_sin, w = lane // LANES_V7X, lane % HEAD_DIM
        if w < ROPE_DIMS:
            select[is_sin * half + w % half, lane] = (-1.0 if w < half else 1.0) if is_sin else 1.0
        elif not is_sin:
            ones_row[0, lane] = 1.0
    return jnp.dot(cos_sin, jnp.asarray(select), precision=lax.Precision.HIGHEST) + jnp.asarray(ones_row)


def _band_mask():
    qi = np.arange(BLOCK)[:, None]
    kj = np.arange(3 * BLOCK)[None, :]
    ok = np.abs(kj - BLOCK - qi) <= BLOCK
    band = np.where(ok, 0.0, NEG_INF).astype(np.float32)
    return jnp.asarray(np.concatenate([band, band], axis=0))


def _nbr_bias_table(rel_bias):
    col = np.arange(GRID_W)
    cs = np.clip(col - NA_KW // 2, 0, GRID_W - NA_KW)
    col_in = (col[None, :] >= cs[:, None]) & (col[None, :] < cs[:, None] + NA_KW)
    dc = np.clip(col[None, :] - col[:, None] + NA_KW - 1, 0, 2 * NA_KW - 2)
    ndc = 2 * NA_KW - 1
    onehot = (dc.reshape(-1)[None, :] == np.arange(ndc)[:, None]).astype(np.float32)
    tm = jnp.einsum("hrd,dx->hrx", rel_bias.astype(jnp.float32), jnp.asarray(onehot),
                    precision=lax.Precision.HIGHEST)
    tm = jnp.where(jnp.asarray(col_in.reshape(-1))[None, None, :], tm * LOG2E, NEG_INF)
    tm = tm.reshape(C_HEADS, 2 * NA_KH - 1, GRID_W, GRID_W)
    pair = jnp.concatenate([tm[:, :-1], tm[:, 1:]], axis=-1)
    pair = pair.reshape(C_HEADS // 2, 2, 2 * NA_KH - 2, GRID_W, 2 * GRID_W)
    return pair.transpose(0, 2, 1, 3, 4).reshape(C_HEADS // 2, 2 * NA_KH - 2, 2 * GRID_W, 2 * GRID_W)


def _layer_params(l, seq_tables, norm1_g, norm2_g, w_in, qa_norm_g, ka_norm_g, a_sink, bv_norm_g,
                  w_spatial, b_spatial, qc_norm_g, kc_norm_g, c_rel_bias, w_out, w_ff1, w_ff2):
    order = np.asarray(A_HEAD_ORDER)
    q_cols = (order[:, None] * HEAD_DIM + np.arange(HEAD_DIM)[None, :]).reshape(-1)
    in_cols = np.concatenate([q_cols, np.arange(A_Q, D_IN)])
    out_rows = np.concatenate([q_cols, np.arange(A_Q, D_MIX)])
    seg = np.kron(np.eye(SEG_W // HEAD_DIM), np.ones((HEAD_DIM, HEAD_DIM))).astype(np.float32)
    ws = w_spatial[l].astype(jnp.bfloat16)
    return {
        "g1": norm1_g[l].reshape(1, D_MODEL),
        "g2": norm2_g[l].reshape(1, D_MODEL),
        "w_in": w_in[l][:, in_cols].astype(jnp.bfloat16),
        "w_out": w_out[l][out_rows, :].astype(jnp.bfloat16),
        "w_ff1": w_ff1[l].astype(jnp.bfloat16),
        "w_ff2": w_ff2[l].astype(jnp.bfloat16),
        "seg": jnp.asarray(seg, jnp.bfloat16),
        "ga": jnp.concatenate([_tile_heads(qa_norm_g[l], A_HEADS, ATTN_SCALE * LOG2E),
                               _tile_heads(ka_norm_g[l], A_KV_HEADS)], axis=1),
        "gc": jnp.concatenate([_tile_heads(qc_norm_g[l], C_HEADS, ATTN_SCALE * LOG2E),
                               _tile_heads(kc_norm_g[l], C_HEADS)], axis=1),
        "gbv": bv_norm_g[l].reshape(1, B_W),
        "rope": seq_tables,
        "ws": jnp.concatenate([ws[0::2], ws[1::2]], axis=2),
        "bs": jnp.repeat(b_spatial[l].T, HEAD_DIM, axis=1),
        "sink": a_sink[l][order].astype(jnp.float32) * LOG2E,
        "nbr_bias": _nbr_bias_table(c_rel_bias[l]),
    }


def _attn_kernel(rows, *refs):
    n_win_in = 9
    n_nbr_in = 2 + 2 * NBR_WIN_BLOCKS
    win_in = refs[:n_win_in]
    nbr_in = refs[n_win_in:n_win_in + n_nbr_in]
    oa_ref, oc_ref, kwin_a, vwin_a, kwin_c, vwin_c = refs[n_win_in + n_nbr_in:]
    _window_kernel(*win_in, oa_ref, kwin_a, vwin_a)
    _nbr_kernel(rows, *nbr_in, oc_ref, kwin_c, vwin_c)


def _attn_call(aq, ak, av, sink, band, cq, ck, cv, bias_tab):
    bsz, seq, _ = aq.shape
    rows = seq // GRID_W
    assert TQ_WIN == TR_NBR * GRID_W and seq % TQ_WIN == 0
    nt = TQ_WIN // BLOCK
    nb = seq // BLOCK
    blk = KV_ROWS_BLK * GRID_W
    nwin = NBR_WIN_BLOCKS
    assert rows >= nwin * KV_ROWS_BLK and NA_KH // 2 <= KV_ROWS_BLK
    cur = lambda w: pl.BlockSpec((None, TQ_WIN, w), lambda b, i: (b, i, 0))
    prev = pl.BlockSpec((None, BLOCK, A_KV), lambda b, i: (b, jnp.maximum(i * nt - 1, 0), 0))
    nxt = pl.BlockSpec((None, BLOCK, A_KV), lambda b, i: (b, jnp.minimum(i * nt + nt, nb - 1), 0))
    kv_specs = [pl.BlockSpec((None, blk, C_W), lambda b, i, n=n: (b, _nbr_window_block(i, rows) + n, 0))
                for n in range(nwin)]
    return pl.pallas_call(
        partial(_attn_kernel, rows),
        grid=(bsz, seq // TQ_WIN),
        in_specs=[
            pl.BlockSpec(memory_space=pltpu.SMEM),
            pl.BlockSpec(band.shape, lambda b, i: (0, 0)),
            cur(A_Q), prev, cur(A_KV), nxt, prev, cur(A_KV), nxt,
            pl.BlockSpec(bias_tab.shape, lambda b, i: (0, 0, 0, 0)), cur(C_W),
        ] + kv_specs + kv_specs,
        out_specs=[cur(A_Q), cur(C_W)],
        out_shape=[jax.ShapeDtypeStruct((bsz, seq, A_Q), jnp.bfloat16),
                   jax.ShapeDtypeStruct((bsz, seq, C_W), jnp.bfloat16)],
        scratch_shapes=[pltpu.VMEM((TQ_WIN + 2 * BLOCK, A_KV), jnp.bfloat16),
                        pltpu.VMEM((TQ_WIN + 2 * BLOCK, A_KV + LANES_V7X), jnp.bfloat16),
                        pltpu.VMEM((nwin * blk, C_W), jnp.bfloat16),
                        pltpu.VMEM((nwin * blk, 2 * C_W), jnp.bfloat16)],
        compiler_params=_cparams(2),
        name="attn_fused",
    )(sink, band, aq, ak, ak, ak, av, av, av, bias_tab, cq, *([ck] * nwin), *([cv] * nwin))


def _trunk(x, mod, boff, layer_params):
    for l, p in enumerate(layer_params):
        aq, ak, av, ob, cq, ck, cv = _inproj_call(x, mod, l, boff, p)
        oa, oc = _attn_call(aq, ak, av, p["sink"], p["band"], cq, ck, cv, p["nbr_bias"])
        x = _ffn_call(x, oa, ob, oc, mod, l, boff, p)
    return x


def kernel(x_prompt, x_sample, c_prompt, c_sample, norm1_g, norm2_g, w_ada, b_ada, w_in, qa_norm_g, ka_norm_g, a_sink, bv_norm_g, w_spatial, b_spatial, qc_norm_g, kc_norm_g, c_rel_bias, w_out, w_ff1, w_ff2):
    depth = w_in.shape[0]
    nb_prompt, seq_prompt, _ = x_prompt.shape
    nb_sample, seq_sample, _ = x_sample.shape
    assert nb_prompt + nb_sample <= 8
    for seq in (seq_prompt, seq_sample):
        assert seq % TM_IN == 0 and seq % TM_FFN == 0 and seq % TQ_WIN == 0 and seq // BLOCK >= 3

    c8 = jnp.zeros((8, D_MODEL), jnp.float32)
    c8 = c8.at[:nb_prompt].set(c_prompt).at[nb_prompt:nb_prompt + nb_sample].set(c_sample)
    mod = _ada_call(c8, w_ada, b_ada).reshape(depth, 8, 6, D_MODEL)

    seq_tables = _rope_tables(max(seq_prompt, seq_sample))
    band = _band_mask()
    layer_params = []
    for l in range(depth):
        p = _layer_params(l, seq_tables, norm1_g, norm2_g, w_in, qa_norm_g, ka_norm_g, a_sink,
                          bv_norm_g, w_spatial, b_spatial, qc_norm_g, kc_norm_g, c_rel_bias,
                          w_out, w_ff1, w_ff2)
        p["band"] = band
        layer_params.append(p)

    y_prompt = _trunk(x_prompt, mod, 0, layer_params)
    y_sample = _trunk(x_sample, mod, nb_prompt, layer_params)
    return (y_prompt, y_sample)
```
